```python
import math
import jax, jax.numpy as jnp
from jax import lax
import numpy as np

D_MODEL = 1024
BATCH = 8
SEQ = 4096
DEPTH = 1

GMLP_WIDTH = D_MODEL
GMLP_GROUPS = 8
GMLP_GROUP_DIM = GMLP_WIDTH // GMLP_GROUPS
GMLP_CHUNK = 128
HGRN_HEADS = 8
HGRN_DK = 128
HGRN_DV = D_MODEL // HGRN_HEADS
HGRN_KEY_WIDTH = HGRN_HEADS * HGRN_DK
HGRN_VAL_WIDTH = HGRN_HEADS * HGRN_DV
HGRN_CHUNK = 64
HGRN_SCALE = HGRN_DK ** -0.5
N_BRANCHES = 2
D_FF = -(-(8 * D_MODEL) // (3 * 256)) * 256
IN_SIZES = (GMLP_WIDTH, GMLP_WIDTH, HGRN_KEY_WIDTH, HGRN_KEY_WIDTH,
            HGRN_VAL_WIDTH, HGRN_VAL_WIDTH, D_MODEL, D_MODEL)
IN_WIDTH = sum(IN_SIZES)
NORM_EPS = 1e-6

kernel_name = "gmlp_hgrn2_gated_hybrid_block"


def _split_points(sizes):
    pts, acc = [], 0
    for s in sizes[:-1]:
        acc += s
        pts.append(acc)
    return pts


def rms_norm(x, gain):
    xf = x.astype(jnp.float32)
    y = xf * lax.rsqrt(jnp.mean(xf * xf, axis=-1, keepdims=True) + NORM_EPS)
    return (y * gain.astype(jnp.float32)).astype(x.dtype)


def layer_norm(x, gain, bias):
    xf = x.astype(jnp.float32)
    mu = jnp.mean(xf, axis=-1, keepdims=True)
    var = jnp.mean(jnp.square(xf - mu), axis=-1, keepdims=True)
    y = (xf - mu) * lax.rsqrt(var + NORM_EPS)
    return (y * gain.astype(jnp.float32) + bias.astype(jnp.float32)).astype(x.dtype)


def gmlp_spatial_gating(u, v, ln_g, ln_b, w_s, b_s):
    B, S, _ = v.shape
    n_chunks = S // GMLP_CHUNK
    v = layer_norm(v, ln_g, ln_b)
    vc = v.reshape(B, n_chunks, GMLP_CHUNK, GMLP_GROUPS, GMLP_GROUP_DIM)
    causal = jnp.tril(jnp.ones((GMLP_CHUNK, GMLP_CHUNK), dtype=bool))
    w = jnp.where(causal, w_s, jnp.zeros((), w_s.dtype)).astype(v.dtype)
    mixed = jnp.einsum('gts,bnsgd->bntgd', w, vc) + b_s.T.astype(v.dtype)[:, :, None]
    return u * mixed.reshape(B, S, GMLP_WIDTH)


def hgrn2_chunkwise(q, f_logit, v, lb):
    B, S, _ = q.shape
    C = HGRN_CHUNK
    n_chunks = S // C
    f32 = jnp.float32
    lb = lb.astype(f32)
    f = lb + (1.0 - lb) * jax.nn.sigmoid(f_logit.astype(f32))
    k = 1.0 - f
    log_f = jnp.log(f)

    def to_chunks(t, d):
        return t.reshape(B, n_chunks, C, HGRN_HEADS, d).transpose(0, 3, 1, 2, 4)

    qc = to_chunks(q.astype(f32), HGRN_DK) * HGRN_SCALE
    kc = to_chunks(k, HGRN_DK)
    vc = to_chunks(v.astype(f32), HGRN_DV)
    A = jnp.cumsum(to_chunks(log_f, HGRN_DK), axis=3)
    A_mid = A[:, :, :, C // 2 - 1:C // 2, :]
    A_last = A[:, :, :, C - 1:C, :]

    q_in = qc * jnp.exp(A - A_mid)
    k_in = kc * jnp.exp(A_mid - A)
    causal = jnp.tril(jnp.ones((C, C), dtype=bool))
    scores = jnp.where(causal, jnp.einsum('bhctk,bhcsk->bhcts', q_in, k_in), 0.0)
    o_intra = jnp.einsum('bhcts,bhcsv->bhctv', scores, vc)

    dS = jnp.einsum('bhcsk,bhcsv->bhckv', kc * jnp.exp(A_last - A), vc)
    decay = jnp.exp(A_last[:, :, :, 0, :])

    def step(S_prev, xs):
        d, ds = xs
        return d[..., None] * S_prev + ds, S_prev

    S0 = jnp.zeros((B, HGRN_HEADS, HGRN_DK, HGRN_DV), f32)
    _, S_before = lax.scan(step, S0, (jnp.moveaxis(decay, 2, 0), jnp.moveaxis(dS, 2, 0)))
    S_before = jnp.moveaxis(S_before, 0, 2)
    o_inter = jnp.einsum('bhctk,bhckv->bhctv', qc * jnp.exp(A), S_before)

    o = o_intra + o_inter
    return o.transpose(0, 2, 3, 1, 4).reshape(B, S, HGRN_HEADS, HGRN_DV)


def setup_inputs(seed: int = 0) -> dict:
    key = jax.random.key(seed)
    ks = jax.random.split(key, 16)
    f32 = jnp.float32

    def normal(k, shape, scale):
        return jax.random.normal(k, shape, f32) * scale

    return {
        "x": normal(ks[0], (BATCH, SEQ, D_MODEL), 1.0),
        "norm_mix_g": 1.0 + normal(ks[1], (DEPTH, D_MODEL), 0.05),
        "w_in": normal(ks[2], (DEPTH, D_MODEL, IN_WIDTH), D_MODEL ** -0.5),
        "gmlp_ln_g": 1.0 + normal(ks[3], (DEPTH, GMLP_WIDTH), 0.05),
        "gmlp_ln_b": normal(ks[4], (DEPTH, GMLP_WIDTH), 0.02),
        "gmlp_w_s": normal(ks[5], (DEPTH, GMLP_GROUPS, GMLP_CHUNK, GMLP_CHUNK), GMLP_CHUNK ** -0.5),
        "gmlp_b_s": 1.0 + normal(ks[6], (DEPTH, GMLP_GROUPS, GMLP_CHUNK), 0.1),
        "hgrn_lb_table": normal(ks[7], (DEPTH + 1, HGRN_KEY_WIDTH), 0.5),
        "hgrn_norm_g": 1.0 + normal(ks[8], (DEPTH, HGRN_DV * HGRN_HEADS), 0.05),
        "w_branch_a": normal(ks[9], (DEPTH, GMLP_WIDTH, D_MODEL), GMLP_WIDTH ** -0.5),
        "w_branch_b": normal(ks[10], (DEPTH, HGRN_VAL_WIDTH, D_MODEL), HGRN_VAL_WIDTH ** -0.5),
        "w_out": normal(ks[11], (DEPTH, D_MODEL, D_MODEL), D_MODEL ** -0.5),
        "norm_ffn_g": 1.0 + normal(ks[12], (DEPTH, D_MODEL), 0.05),
        "w_gate_up": normal(ks[13], (DEPTH, D_MODEL, 2 * D_FF), D_MODEL ** -0.5),
        "w_down": normal(ks[14], (DEPTH, D_FF, D_MODEL), D_FF ** -0.5),
        "norm_final_g": 1.0 + normal(ks[15], (D_MODEL,), 0.05),
    }


def reference(x, norm_mix_g, w_in, gmlp_ln_g, gmlp_ln_b, gmlp_w_s, gmlp_b_s,
              hgrn_lb_table, hgrn_norm_g, w_branch_a, w_branch_b, w_out,
              norm_ffn_g, w_gate_up, w_down, norm_final_g):
    B, S, _ = x.shape
    split_pts = _split_points(IN_SIZES)
    lb_all = jnp.cumsum(jax.nn.softmax(hgrn_lb_table.astype(jnp.float32), axis=0), axis=0)

    for l in range(DEPTH):
        h = rms_norm(x, norm_mix_g[l])
        proj = jnp.einsum('bsd,de->bse', h, w_in[l])
        u, v, q, f_logit, i_val, g_out, gate_a, gate_b = jnp.split(proj, split_pts, axis=-1)

        a = gmlp_spatial_gating(jax.nn.gelu(u), jax.nn.gelu(v), gmlp_ln_g[l], gmlp_ln_b[l],
                                gmlp_w_s[l], gmlp_b_s[l])
        y_a = jnp.einsum('bse,ed->bsd', a, w_branch_a[l])

        o = hgrn2_chunkwise(q, f_logit, i_val, lb_all[l])
        o = o * lax.rsqrt(jnp.mean(o * o, axis=-1, keepdims=True) + NORM_EPS)
        o = (o.reshape(B, S, HGRN_VAL_WIDTH) * hgrn_norm_g[l].astype(jnp.float32)).astype(x.dtype)
        o = o * jax.nn.silu(g_out)
        y_b = jnp.einsum('bse,ed->bsd', o, w_branch_b[l])

        merged = jax.nn.sigmoid(gate_a) * y_a + jax.nn.sigmoid(gate_b) * y_b
        x = x + jnp.einsum('bsd,de->bse', merged, w_out[l])

        h = rms_norm(x, norm_ffn_g[l])
        gu = jnp.einsum('bsd,df->bsf', h, w_gate_up[l])
        gate, up = jnp.split(gu, [D_FF], axis=-1)
        x = x + jnp.einsum('bsf,fd->bsd', jax.nn.silu(gate) * up, w_down[l])

    return rms_norm(x, norm_final_g)
```

```python
import functools
import math

import jax
import jax.numpy as jnp
from jax import lax
from jax.experimental import pallas as pl
from jax.experimental.pallas import tpu as pltpu

F32 = jnp.float32
BF16 = jnp.bfloat16

NORM_EPS = 1e-6
GMLP_GROUPS = 8
GMLP_CHUNK = 128
HGRN_HEADS = 8
HGRN_DK = 128
HGRN_DV = 128
HGRN_CHUNK = 64
HGRN_SCALE = HGRN_DK ** -0.5
N_IN_SEGMENTS = 8

MIXER_SEQ_TILE = 256
FFN_TOKEN_TILE = 512
FFN_COL_CHUNK = 256
V7X_VMEM_LIMIT_BYTES = 56 * 1024 * 1024


def _sigmoid(x):
    return 0.5 * (1.0 + jnp.tanh(0.5 * x))


def _gelu_tanh(x):
    c = math.sqrt(2.0 / math.pi)
    return 0.5 * x * (1.0 + jnp.tanh(c * (x + 0.044715 * (x * x * x))))


def _rms_norm(x, gain):
    return x * lax.rsqrt(jnp.mean(x * x, axis=-1, keepdims=True) + NORM_EPS) * gain


def _dot(a, b):
    return jnp.dot(a, b, preferred_element_type=F32)


def _dot_nt(a, b):
    return lax.dot_general(a, b, (((1,), (1,)), ((), ())), preferred_element_type=F32)


def _dot_tn(a, b):
    return lax.dot_general(a, b, (((0,), (0,)), ((), ())), preferred_element_type=F32)


def _mixer_kernel(x_ref, nmg_ref, win_ref, lng_ref, lnb_ref, ws_ref, bst_ref, lbt_ref,
                  hng_ref, wa_ref, wb_ref, wo_ref, o_ref, state_ref, a_ref, og_ref):
    T, D = x_ref.shape
    seg = win_ref.shape[1] // N_IN_SEGMENTS

    @pl.when(pl.program_id(1) == 0)
    def _():
        state_ref[...] = jnp.zeros_like(state_ref)

    x = x_ref[...]
    h = _rms_norm(x, nmg_ref[...]).astype(BF16)

    def proj(i):
        return _dot(h, win_ref[:, i * seg:(i + 1) * seg])

    row = lax.broadcasted_iota(jnp.int32, (T, T), 0)
    col = lax.broadcasted_iota(jnp.int32, (T, T), 1)

    gu = _gelu_tanh(proj(0))
    gv = _gelu_tanh(proj(1))
    mu = jnp.mean(gv, axis=-1, keepdims=True)
    cen = gv - mu
    var = jnp.mean(cen * cen, axis=-1, keepdims=True)
    lnv = (cen * lax.rsqrt(var + NORM_EPS) * lng_ref[...] + lnb_ref[...]).astype(BF16)
    c_row = lax.broadcasted_iota(jnp.int32, (GMLP_CHUNK, GMLP_CHUNK), 0)
    c_col = lax.broadcasted_iota(jnp.int32, (GMLP_CHUNK, GMLP_CHUNK), 1)
    gd = seg // GMLP_GROUPS
    for g in range(GMLP_GROUPS):
        w_g = jnp.where(c_col <= c_row, ws_ref[g], 0.0).astype(BF16)
        bias = bst_ref[:, g:g + 1]
        for n in range(T // GMLP_CHUNK):
            rs = slice(n * GMLP_CHUNK, (n + 1) * GMLP_CHUNK)
            cs = slice(g * gd, (g + 1) * gd)
            mixed = _dot(w_g, lnv[rs, cs]) + bias
            a_ref[rs, cs] = (gu[rs, cs] * mixed).astype(BF16)
    y_a = _dot(a_ref[...], wa_ref[...])

    lbt = lbt_ref[...]
    lbe = jnp.exp(lbt - jnp.max(lbt, axis=0, keepdims=True))
    lb = lbe[0:1, :] / jnp.sum(lbe, axis=0, keepdims=True)

    q = proj(2) * HGRN_SCALE
    f = lb + (1.0 - lb) * _sigmoid(proj(3))
    kk = 1.0 - f
    log_f = jnp.log(f)
    same_chunk = (row >> 6) == (col >> 6)
    causal = jnp.logical_and(same_chunk, col <= row)
    tri = jnp.where(causal, 1.0, 0.0).astype(BF16)
    lf_hi = log_f.astype(BF16)
    lf_lo = (log_f - lf_hi.astype(F32)).astype(BF16)
    A = _dot(tri, lf_hi) + _dot(tri, lf_lo)

    C = HGRN_CHUNK
    n_chunks = T // C
    a_mid_rows, e_mid_rows, e_lm_rows, decay_rows = [], [], [], []
    for c in range(n_chunks):
        a_mid = A[c * C + C // 2 - 1:c * C + C // 2, :]
        a_last = A[c * C + C - 1:c * C + C, :]
        a_mid_rows.append(jnp.broadcast_to(a_mid, (C, seg)))
        e_mid_rows.append(jnp.broadcast_to(jnp.exp(a_mid), (C, seg)))
        e_lm_rows.append(jnp.broadcast_to(jnp.exp(a_last - a_mid), (C, seg)))
        decay_rows.append(jnp.exp(a_last))
    a_mid_b = jnp.concatenate(a_mid_rows, axis=0)
    rel = A - a_mid_b
    q_in = q * jnp.exp(rel)
    k_in = kk * jnp.exp(-rel)
    q_dec = (q_in * jnp.concatenate(e_mid_rows, axis=0)).astype(BF16)
    k_dec = (k_in * jnp.concatenate(e_lm_rows, axis=0)).astype(BF16)
    q_in = q_in.astype(BF16)
    k_in = k_in.astype(BF16)
    v_b = proj(4).astype(BF16)
    g_out = proj(5)
    g_silu = g_out * _sigmoid(g_out)
    hng = hng_ref[...]

    for hd in range(HGRN_HEADS):
        ks = slice(hd * HGRN_DK, (hd + 1) * HGRN_DK)
        vs = slice(hd * HGRN_DV, (hd + 1) * HGRN_DV)
        v_h = v_b[:, vs]
        scores = jnp.where(causal, _dot_nt(q_in[:, ks], k_in[:, ks]), 0.0).astype(BF16)
        o_intra = _dot(scores, v_h)
        st = state_ref[hd]
        o_rows = []
        for c in range(n_chunks):
            rs = slice(c * C, (c + 1) * C)
            o_rows.append(o_intra[rs, :] + _dot_nt(q_dec[rs, ks], st.astype(BF16)))
            st = st * decay_rows[c][:, ks] + _dot_tn(v_h[rs, :], k_dec[rs, ks])
        state_ref[hd] = st
        o_h = jnp.concatenate(o_rows, axis=0)
        o_h = o_h * lax.rsqrt(jnp.mean(o_h * o_h, axis=-1, keepdims=True) + NORM_EPS)
        og_ref[:, vs] = (o_h * hng[:, vs] * g_silu[:, vs]).astype(BF16)
    y_b = _dot(og_ref[...], wb_ref[...])

    merged = (_sigmoid(proj(6)) * y_a + _sigmoid(proj(7)) * y_b).astype(BF16)
    o_ref[...] = x + _dot(merged, wo_ref[...])


def _ffn_kernel(x_ref, nfg_ref, wgu_ref, wd_ref, nog_ref, o_ref, act_ref):
    d_ff = wd_ref.shape[0]
    x = x_ref[...]
    h = _rms_norm(x, nfg_ref[...]).astype(BF16)
    for j in range(d_ff // FFN_COL_CHUNK):
        cs = slice(j * FFN_COL_CHUNK, (j + 1) * FFN_COL_CHUNK)
        gate = _dot(h, wgu_ref[:, cs])
        up = _dot(h, wgu_ref[:, d_ff + j * FFN_COL_CHUNK:d_ff + (j + 1) * FFN_COL_CHUNK])
        act_ref[:, cs] = (gate * _sigmoid(gate) * up).astype(BF16)
    x2 = x + _dot(act_ref[...], wd_ref[...])
    o_ref[...] = _rms_norm(x2, nog_ref[...])


def _resident(shape):
    n = len(shape)
    return pl.BlockSpec(shape, lambda *_: (0,) * n, pipeline_mode=pl.Buffered(1))


def _mixer_call(x, nmg, w_in, lng, lnb, w_s, bst, lbt, hng, w_a, w_b, w_o):
    B, S, D = x.shape
    T = MIXER_SEQ_TILE
    seg = w_in.shape[1] // N_IN_SEGMENTS
    x_spec = pl.BlockSpec((None, T, D), lambda b, s: (b, s, 0))
    params = (nmg, w_in, lng, lnb, w_s, bst, lbt, hng, w_a, w_b, w_o)
    return pl.pallas_call(
        _mixer_kernel,
        grid=(B, S // T),
        in_specs=[x_spec] + [_resident(p.shape) for p in params],
        out_specs=x_spec,
        out_shape=jax.ShapeDtypeStruct(x.shape, x.dtype),
        scratch_shapes=[
            pltpu.VMEM((HGRN_HEADS, HGRN_DV, HGRN_DK), F32),
            pltpu.VMEM((T, seg), BF16),
            pltpu.VMEM((T, seg), BF16),
        ],
        compiler_params=pltpu.CompilerParams(
            dimension_semantics=("arbitrary", "arbitrary"),
            vmem_limit_bytes=V7X_VMEM_LIMIT_BYTES),
        name="mixer",
    )(x, *params)


def _ffn_call(x, nfg, w_gu, w_d, nog):
    N, D = x.shape
    T = FFN_TOKEN_TILE
    x_spec = pl.BlockSpec((T, D), lambda i: (i, 0))
    params = (nfg, w_gu, w_d, nog)
    return pl.pallas_call(
        _ffn_kernel,
        grid=(N // T,),
        in_specs=[x_spec] + [_resident(p.shape) for p in params],
        out_specs=x_spec,
        out_shape=jax.ShapeDtypeStruct(x.shape, x.dtype),
        scratch_shapes=[pltpu.VMEM((T, w_d.shape[0]), BF16)],
        compiler_params=pltpu.CompilerParams(
            dimension_semantics=("arbitrary",),
            vmem_limit_bytes=V7X_VMEM_LIMIT_BYTES),
        name="ffn",
    )(x, *params)


@jax.jit
def kernel(x, norm_mix_g, w_in, gmlp_ln_g, gmlp_ln_b, gmlp_w_s, gmlp_b_s, hgrn_lb_table,
           hgrn_norm_g, w_branch_a, w_branch_b, w_out, norm_ffn_g, w_gate_up, w_down,
           norm_final_g):
    B, S, D = x.shape
    depth = w_in.shape[0]
    assert depth == 1 and hgrn_lb_table.shape[0] == depth + 1
    assert S % MIXER_SEQ_TILE == 0 and (B * S) % FFN_TOKEN_TILE == 0
    assert w_down.shape[1] % FFN_COL_CHUNK == 0
    l = 0
    x1 = _mixer_call(
        x, norm_mix_g[l][None, :], w_in[l].astype(BF16),
        gmlp_ln_g[l][None, :], gmlp_ln_b[l][None, :], gmlp_w_s[l], gmlp_b_s[l].T,
        hgrn_lb_table, hgrn_norm_g[l][None, :],
        w_branch_a[l].astype(BF16), w_branch_b[l].astype(BF16), w_out[l].astype(BF16))
    out = _ffn_call(
        x1.reshape(B * S, D), norm_ffn_g[l][None, :], w_gate_up[l].astype(BF16),
        w_down[l].astype(BF16), norm_final_g[None, :])
    return out.reshape(B, S, D)
```

```python
import math

import jax
import jax.numpy as jnp
from jax import lax
from jax.experimental import pallas as pl
from jax.experimental.pallas import tpu as pltpu

F32 = jnp.float32
BF16 = jnp.bfloat16

NORM_EPS = 1e-6
GMLP_GROUPS = 8
GMLP_CHUNK = 128
HGRN_HEADS = 8
HGRN_DK = 128
HGRN_DV = 128
HGRN_CHUNK = 64
HGRN_SCALE = HGRN_DK ** -0.5
N_IN_SEGMENTS = 8

MIXER_SEQ_TILE = 256
FFN_TOKEN_TILE = 512
FFN_COL_CHUNK = 256
MIXER_COL_SLAB = 512
V7X_VMEM_LIMIT_BYTES = 56 * 1024 * 1024


def _sigmoid(x):
    return 0.5 * (1.0 + jnp.tanh(0.5 * x))


def _gelu_tanh(x):
    c = math.sqrt(2.0 / math.pi)
    return 0.5 * x * (1.0 + jnp.tanh(c * (x + 0.044715 * (x * x * x))))


def _rms_norm(x, gain):
    return x * lax.rsqrt(jnp.mean(x * x, axis=-1, keepdims=True) + NORM_EPS) * gain


def _dot(a, b):
    return jnp.dot(a, b, preferred_element_type=F32)


def _dot_nt(a, b):
    return lax.dot_general(a, b, (((1,), (1,)), ((), ())), preferred_element_type=F32)


def _dot_tn(a, b):
    return lax.dot_general(a, b, (((0,), (0,)), ((), ())), preferred_element_type=F32)


def _pack_rows(w):
    k, n = w.shape
    pairs = w.astype(BF16).reshape(k // 2, 2, n).transpose(0, 2, 1)
    return lax.bitcast_convert_type(pairs, jnp.uint32)


def _unpack_rows(packed):
    return pltpu.bitcast(packed, BF16)


def _mixer_kernel(x_ref, nmg_ref, win_ref, lng_ref, lnb_ref, ws_ref, bst_ref, lbt_ref,
                  hng_ref, wa_ref, wb_ref, wo_ref, o_ref,
                  state_ref, qin_ref, kin_ref, qdec_ref, kdec_ref, v_ref, decay_ref, gs_ref,
                  gu_ref, gv_ref, sga_ref, sgb_ref, a_ref, og_ref, mg_ref):
    T, D = x_ref.shape
    seg = win_ref.shape[1] // N_IN_SEGMENTS
    W = MIXER_COL_SLAB
    n_slabs = seg // W
    C = HGRN_CHUNK
    n_chunks = T // C

    @pl.when(pl.program_id(1) == 0)
    def _():
        state_ref[...] = jnp.zeros_like(state_ref)

    h = _rms_norm(x_ref[...], nmg_ref[...]).astype(BF16)

    def proj(i, j):
        c0 = i * seg + j * W
        return _dot(h, _unpack_rows(win_ref[:, c0:c0 + W]))

    row = lax.broadcasted_iota(jnp.int32, (T, T), 0)
    col = lax.broadcasted_iota(jnp.int32, (T, T), 1)
    causal = jnp.logical_and((row >> 6) == (col >> 6), col <= row)
    tri = jnp.where(causal, 1.0, 0.0).astype(BF16)

    lbt = lbt_ref[...]
    lbe = jnp.exp(lbt - jnp.max(lbt, axis=0, keepdims=True))
    lb = lbe[0:1, :] / jnp.sum(lbe, axis=0, keepdims=True)

    gv_sum = jnp.zeros((T, 1), F32)
    f_logits = [proj(3, j) for j in range(n_slabs)]
    gate_vals = []
    for j in range(n_slabs):
        lb_j = lb[:, j * W:(j + 1) * W]
        f = lb_j + (1.0 - lb_j) * _sigmoid(f_logits[j])
        log_f = jnp.log(f)
        lf_hi = log_f.astype(BF16)
        lf_lo = (log_f - lf_hi.astype(F32)).astype(BF16)
        gate_vals.append((1.0 - f, _dot(tri, lf_hi) + _dot(tri, lf_lo), proj(2, j) * HGRN_SCALE))
    for j in range(n_slabs):
        js = slice(j * W, (j + 1) * W)
        kk, A, q = gate_vals[j]
        a_mid_rows, e_mid_rows, e_lm_rows = [], [], []
        for c in range(n_chunks):
            a_mid = A[c * C + C // 2 - 1:c * C + C // 2, :]
            a_last = A[c * C + C - 1:c * C + C, :]
            a_mid_rows.append(jnp.broadcast_to(a_mid, (C, W)))
            e_mid_rows.append(jnp.broadcast_to(jnp.exp(a_mid), (C, W)))
            e_lm_rows.append(jnp.broadcast_to(jnp.exp(a_last - a_mid), (C, W)))
            decay_ref[c:c + 1, js] = jnp.exp(a_last)
        rel = A - jnp.concatenate(a_mid_rows, axis=0)
        q_in = q * jnp.exp(rel)
        k_in = kk * jnp.exp(-rel)
        qdec_ref[:, js] = (q_in * jnp.concatenate(e_mid_rows, axis=0)).astype(BF16)
        kdec_ref[:, js] = (k_in * jnp.concatenate(e_lm_rows, axis=0)).astype(BF16)
        qin_ref[:, js] = q_in.astype(BF16)
        kin_ref[:, js] = k_in.astype(BF16)
        v_ref[:, js] = proj(4, j).astype(BF16)
        g_out = proj(5, j)
        gs_ref[:, js] = g_out * _sigmoid(g_out)
        gu_ref[:, js] = _gelu_tanh(proj(0, j))
        gv = _gelu_tanh(proj(1, j))
        gv_ref[:, js] = gv
        gv_sum = gv_sum + jnp.sum(gv, axis=-1, keepdims=True)

    gate_slabs = [(sga_ref, 6, j) for j in range(n_slabs)] + [(sgb_ref, 7, j) for j in range(n_slabs)]

    def next_gate_slab():
        ref, i, j = gate_slabs.pop(0)
        ref[:, j * W:(j + 1) * W] = _sigmoid(proj(i, j))

    next_gate_slab()
    mu = gv_sum * (1.0 / seg)
    sq_sum = jnp.zeros((T, 1), F32)
    for j in range(n_slabs):
        cen = gv_ref[:, j * W:(j + 1) * W] - mu
        sq_sum = sq_sum + jnp.sum(cen * cen, axis=-1, keepdims=True)
    rstd = lax.rsqrt(sq_sum * (1.0 / seg) + NORM_EPS)
    c_row = lax.broadcasted_iota(jnp.int32, (GMLP_CHUNK, GMLP_CHUNK), 0)
    c_col = lax.broadcasted_iota(jnp.int32, (GMLP_CHUNK, GMLP_CHUNK), 1)
    gd = seg // GMLP_GROUPS

    head_vals = {}

    def head_scores(hd):
        ks = slice(hd * HGRN_DK, (hd + 1) * HGRN_DK)
        vs = slice(hd * HGRN_DV, (hd + 1) * HGRN_DV)
        raw = _dot_nt(qin_ref[:, ks], kin_ref[:, ks])
        d_state = [_dot_tn(v_ref[c * C:(c + 1) * C, vs], kdec_ref[c * C:(c + 1) * C, ks])
                   for c in range(n_chunks)]
        head_vals[hd] = (raw, d_state)

    def head_outputs(hd):
        ks = slice(hd * HGRN_DK, (hd + 1) * HGRN_DK)
        vs = slice(hd * HGRN_DV, (hd + 1) * HGRN_DV)
        raw, d_state = head_vals[hd]
        scores = jnp.where(causal, raw, 0.0).astype(BF16)
        o_intra = _dot(scores, v_ref[:, vs])
        st = state_ref[hd]
        o_inter = []
        for c in range(n_chunks):
            o_inter.append(_dot_nt(qdec_ref[c * C:(c + 1) * C, ks], st.astype(BF16)))
            st = st * decay_ref[c:c + 1, ks] + d_state[c]
        state_ref[hd] = st
        head_vals[hd] = (o_intra, o_inter)

    def head_norm(hd):
        vs = slice(hd * HGRN_DV, (hd + 1) * HGRN_DV)
        o_intra, o_inter = head_vals.pop(hd)
        o_h = o_intra + jnp.concatenate(o_inter, axis=0)
        o_h = o_h * lax.rsqrt(jnp.mean(o_h * o_h, axis=-1, keepdims=True) + NORM_EPS)
        og_ref[:, vs] = (o_h * hng_ref[:, vs] * gs_ref[:, vs]).astype(BF16)

    def gmlp_group(g):
        cs = slice(g * gd, (g + 1) * gd)
        w_g = jnp.where(c_col <= c_row, ws_ref[g], 0.0).astype(BF16)
        bias = bst_ref[:, g:g + 1]
        lnv = ((gv_ref[:, cs] - mu) * rstd * lng_ref[:, cs] + lnb_ref[:, cs]).astype(BF16)
        for n in range(T // GMLP_CHUNK):
            rs = slice(n * GMLP_CHUNK, (n + 1) * GMLP_CHUNK)
            mixed = _dot(w_g, lnv[rs, :]) + bias
            a_ref[rs, cs] = (gu_ref[rs, cs] * mixed).astype(BF16)

    for it in range(HGRN_HEADS + 2):
        if it < HGRN_HEADS:
            head_scores(it)
        if 0 <= it - 1 < HGRN_HEADS:
            head_outputs(it - 1)
        if 0 <= it - 2 < HGRN_HEADS:
            head_norm(it - 2)
        if it < GMLP_GROUPS:
            gmlp_group(it)
        if it % 2 == 1 and gate_slabs:
            next_gate_slab()
    assert not gate_slabs and not head_vals

    for j in range(D // W):
        js = slice(j * W, (j + 1) * W)
        y_a = _dot(a_ref[...], _unpack_rows(wa_ref[:, js]))
        y_b = _dot(og_ref[...], _unpack_rows(wb_ref[:, js]))
        mg_ref[:, js] = (sga_ref[:, js] * y_a + sgb_ref[:, js] * y_b).astype(BF16)
    for j in range(D // W):
        js = slice(j * W, (j + 1) * W)
        o_ref[:, js] = x_ref[:, js] + _dot(mg_ref[...], _unpack_rows(wo_ref[:, js]))


def _ffn_kernel(x_ref, nfg_ref, wgu_ref, wd_ref, nog_ref, o_ref, act_ref):
    d_ff = wgu_ref.shape[1] // 2
    x = x_ref[...]
    h = _rms_norm(x, nfg_ref[...]).astype(BF16)
    for j in range(d_ff // FFN_COL_CHUNK):
        cs = slice(j * FFN_COL_CHUNK, (j + 1) * FFN_COL_CHUNK)
        gate = _dot(h, _unpack_rows(wgu_ref[:, cs]))
        up = _dot(h, _unpack_rows(wgu_ref[:, d_ff + j * FFN_COL_CHUNK:d_ff + (j + 1) * FFN_COL_CHUNK]))
        act_ref[:, cs] = (gate * _sigmoid(gate) * up).astype(BF16)
    x2 = x + _dot(act_ref[...], _unpack_rows(wd_ref[...]))
    o_ref[...] = _rms_norm(x2, nog_ref[...])


def _resident(shape):
    n = len(shape)
    return pl.BlockSpec(shape, lambda *_: (0,) * n, pipeline_mode=pl.Buffered(1))


def _mixer_call(x, nmg, w_in, lng, lnb, w_s, bst, lbt, hng, w_a, w_b, w_o):
    B, S, D = x.shape
    T = MIXER_SEQ_TILE
    seg = w_in.shape[1] // N_IN_SEGMENTS
    x_spec = pl.BlockSpec((None, T, D), lambda b, s: (b, s, 0))
    params = (nmg, w_in, lng, lnb, w_s, bst, lbt, hng, w_a, w_b, w_o)
    return pl.pallas_call(
        _mixer_kernel,
        grid=(B, S // T),
        in_specs=[x_spec] + [_resident(p.shape) for p in params],
        out_specs=x_spec,
        out_shape=jax.ShapeDtypeStruct(x.shape, x.dtype),
        scratch_shapes=[
            pltpu.VMEM((HGRN_HEADS, HGRN_DV, HGRN_DK), F32),
            pltpu.VMEM((T, seg), BF16),
            pltpu.VMEM((T, seg), BF16),
            pltpu.VMEM((T, seg), BF16),
            pltpu.VMEM((T, seg), BF16),
            pltpu.VMEM((T, seg), BF16),
            pltpu.VMEM((T // HGRN_CHUNK, seg), F32),
            pltpu.VMEM((T, seg), F32),
            pltpu.VMEM((T, seg), F32),
            pltpu.VMEM((T, seg), F32),
            pltpu.VMEM((T, D), F32),
            pltpu.VMEM((T, D), F32),
            pltpu.VMEM((T, seg), BF16),
            pltpu.VMEM((T, seg), BF16),
            pltpu.VMEM((T, D), BF16),
        ],
        compiler_params=pltpu.CompilerParams(
            dimension_semantics=("arbitrary", "arbitrary"),
            vmem_limit_bytes=V7X_VMEM_LIMIT_BYTES),
        name="mixer",
    )(x, *params)


def _ffn_call(x, nfg, w_gu, w_d, nog):
    N, D = x.shape
    T = FFN_TOKEN_TILE
    x_spec = pl.BlockSpec((T, D), lambda i: (i, 0))
    params = (nfg, w_gu, w_d, nog)
    return pl.pallas_call(
        _ffn_kernel,
        grid=(N // T,),
        in_specs=[x_spec] + [_resident(p.shape) for p in params],
        out_specs=x_spec,
        out_shape=jax.ShapeDtypeStruct(x.shape, x.dtype),
        scratch_shapes=[pltpu.VMEM((T, w_gu.shape[1] // 2), BF16)],
        compiler_params=pltpu.CompilerParams(
            dimension_semantics=("arbitrary",),
            vmem_limit_bytes=V7X_VMEM_LIMIT_BYTES),
        name="ffn",
    )(x, *params)


@jax.jit
def kernel(x, norm_mix_g, w_in, gmlp_ln_g, gmlp_ln_b, gmlp_w_s, gmlp_b_s, hgrn_lb_table,
           hgrn_norm_g, w_branch_a, w_branch_b, w_out, norm_ffn_g, w_gate_up, w_down,
           norm_final_g):
    B, S, D = x.shape
    depth = w_in.shape[0]
    assert depth == 1 and hgrn_lb_table.shape[0] == depth + 1
    assert S % MIXER_SEQ_TILE == 0 and (B * S) % FFN_TOKEN_TILE == 0
    assert w_down.shape[1] % FFN_COL_CHUNK == 0 and D % MIXER_COL_SLAB == 0
    l = 0
    x1 = _mixer_call(
        x, norm_mix_g[l][None, :], _pack_rows(w_in[l]),
        gmlp_ln_g[l][None, :], gmlp_ln_b[l][None, :], gmlp_w_s[l], gmlp_b_s[l].T,
        hgrn_lb_table, hgrn_norm_g[l][None, :],
        _pack_rows(w_branch_a[l]), _pack_rows(w_branch_b[l]), _pack_rows(w_out[l]))
    out = _ffn_call(
        x1.reshape(B * S, D), norm_ffn_g[l][None, :], _pack_rows(w_gate_up[l]),
        _pack_rows(w_down[l]), norm_final_g[None, :])
    return out.reshape(B, S, D)
```

```python
import math

import jax
import jax.numpy as jnp
from jax import lax
from jax.experimental import pallas as pl
from jax.experimental.pallas import tpu as pltpu

F32 = jnp.float32
BF16 = jnp.bfloat16

NORM_EPS = 1e-6
GMLP_GROUPS = 8
GMLP_CHUNK = 128
HGRN_HEADS = 8
HGRN_DK = 128
HGRN_DV = 128
HGRN_CHUNK = 64
HGRN_SCALE = HGRN_DK ** -0.5
N_IN_SEGMENTS = 8

MIXER_SEQ_TILE = 256
FFN_TOKEN_TILE = 512
FFN_COL_CHUNK = 256
MIXER_COL_SLAB = 512
V7X_VMEM_LIMIT_BYTES = 56 * 1024 * 1024


def _sigmoid(x):
    return 0.5 * (1.0 + jnp.tanh(0.5 * x))


def _gelu_tanh(x):
    c = math.sqrt(2.0 / math.pi)
    return 0.5 * x * (1.0 + jnp.tanh(c * (x + 0.044715 * (x * x * x))))


def _rms_norm(x, gain):
    return x * lax.rsqrt(jnp.mean(x * x, axis=-1, keepdims=True) + NORM_EPS) * gain


def _dot(a, b):
    return jnp.dot(a, b, preferred_element_type=F32)


def _dot_nt(a, b):
    return lax.dot_general(a, b, (((1,), (1,)), ((), ())), preferred_element_type=F32)


def _dot_tn(a, b):
    return lax.dot_general(a, b, (((0,), (0,)), ((), ())), preferred_element_type=F32)


def _mixer_kernel(x_ref, nmg_ref, win_ref, lng_ref, lnb_ref, ws_ref, bst_ref, lbt_ref,
                  hng_ref, wa_ref, wb_ref, wo_ref, o_ref,
                  state_ref, qin_ref, kin_ref, qdec_ref, kdec_ref, v_ref, decay_ref, gs_ref,
                  gu_ref, gv_ref, sga_ref, sgb_ref, a_ref, og_ref, mg_ref):
    T, D = x_ref.shape
    seg = win_ref.shape[1] // N_IN_SEGMENTS
    W = MIXER_COL_SLAB
    n_slabs = seg // W
    C = HGRN_CHUNK
    n_chunks = T // C

    @pl.when(pl.program_id(1) == 0)
    def _():
        state_ref[...] = jnp.zeros_like(state_ref)

    h = _rms_norm(x_ref[...], nmg_ref[...]).astype(BF16)

    def proj(i, j):
        c0 = i * seg + j * W
        return _dot(h, win_ref[:, c0:c0 + W])

    row = lax.broadcasted_iota(jnp.int32, (T, T), 0)
    col = lax.broadcasted_iota(jnp.int32, (T, T), 1)
    causal = jnp.logical_and((row >> 6) == (col >> 6), col <= row)
    tri = jnp.where(causal, 1.0, 0.0).astype(BF16)

    lbt = lbt_ref[...]
    lbe = jnp.exp(lbt - jnp.max(lbt, axis=0, keepdims=True))
    lb = lbe[0:1, :] / jnp.sum(lbe, axis=0, keepdims=True)

    gv_sum = jnp.zeros((T, 1), F32)
    f_logits = [proj(3, j) for j in range(n_slabs)]
    gate_vals = []
    for j in range(n_slabs):
        lb_j = lb[:, j * W:(j + 1) * W]
        f = lb_j + (1.0 - lb_j) * _sigmoid(f_logits[j])
        log_f = jnp.log(f)
        lf_hi = log_f.astype(BF16)
        lf_lo = (log_f - lf_hi.astype(F32)).astype(BF16)
        gate_vals.append((1.0 - f, _dot(tri, lf_hi) + _dot(tri, lf_lo), proj(2, j) * HGRN_SCALE))
    for j in range(n_slabs):
        js = slice(j * W, (j + 1) * W)
        kk, A, q = gate_vals[j]
        a_mid_rows, e_mid_rows, e_lm_rows = [], [], []
        for c in range(n_chunks):
            a_mid = A[c * C + C // 2 - 1:c * C + C // 2, :]
            a_last = A[c * C + C - 1:c * C + C, :]
            a_mid_rows.append(jnp.broadcast_to(a_mid, (C, W)))
            e_mid_rows.append(jnp.broadcast_to(jnp.exp(a_mid), (C, W)))
            e_lm_rows.append(jnp.broadcast_to(jnp.exp(a_last - a_mid), (C, W)))
            decay_ref[c:c + 1, js] = jnp.exp(a_last)
        rel = A - jnp.concatenate(a_mid_rows, axis=0)
        q_in = q * jnp.exp(rel)
        k_in = kk * jnp.exp(-rel)
        qdec_ref[:, js] = (q_in * jnp.concatenate(e_mid_rows, axis=0)).astype(BF16)
        kdec_ref[:, js] = (k_in * jnp.concatenate(e_lm_rows, axis=0)).astype(BF16)
        qin_ref[:, js] = q_in.astype(BF16)
        kin_ref[:, js] = k_in.astype(BF16)
        v_ref[:, js] = proj(4, j).astype(BF16)
        g_out = proj(5, j)
        gs_ref[:, js] = g_out * _sigmoid(g_out)
        gu_ref[:, js] = _gelu_tanh(proj(0, j))
        gv = _gelu_tanh(proj(1, j))
        gv_ref[:, js] = gv
        gv_sum = gv_sum + jnp.sum(gv, axis=-1, keepdims=True)

    gate_slabs = [(sga_ref, 6, j) for j in range(n_slabs)] + [(sgb_ref, 7, j) for j in range(n_slabs)]

    def next_gate_slab():
        ref, i, j = gate_slabs.pop(0)
        ref[:, j * W:(j + 1) * W] = _sigmoid(proj(i, j))

    next_gate_slab()
    mu = gv_sum * (1.0 / seg)
    sq_sum = jnp.zeros((T, 1), F32)
    for j in range(n_slabs):
        cen = gv_ref[:, j * W:(j + 1) * W] - mu
        sq_sum = sq_sum + jnp.sum(cen * cen, axis=-1, keepdims=True)
    rstd = lax.rsqrt(sq_sum * (1.0 / seg) + NORM_EPS)
    c_row = lax.broadcasted_iota(jnp.int32, (GMLP_CHUNK, GMLP_CHUNK), 0)
    c_col = lax.broadcasted_iota(jnp.int32, (GMLP_CHUNK, GMLP_CHUNK), 1)
    gd = seg // GMLP_GROUPS

    head_vals = {}

    def head_scores(hd):
        ks = slice(hd * HGRN_DK, (hd + 1) * HGRN_DK)
        vs = slice(hd * HGRN_DV, (hd + 1) * HGRN_DV)
        raw = _dot_nt(qin_ref[:, ks], kin_ref[:, ks])
        d_state = [_dot_tn(v_ref[c * C:(c + 1) * C, vs], kdec_ref[c * C:(c + 1) * C, ks])
                   for c in range(n_chunks)]
        head_vals[hd] = (raw, d_state)

    def head_outputs(hd):
        ks = slice(hd * HGRN_DK, (hd + 1) * HGRN_DK)
        vs = slice(hd * HGRN_DV, (hd + 1) * HGRN_DV)
        raw, d_state = head_vals[hd]
        scores = jnp.where(causal, raw, 0.0).astype(BF16)
        o_intra = _dot(scores, v_ref[:, vs])
        st = state_ref[hd]
        o_inter = []
        for c in range(n_chunks):
            o_inter.append(_dot_nt(qdec_ref[c * C:(c + 1) * C, ks], st.astype(BF16)))
            st = st * decay_ref[c:c + 1, ks] + d_state[c]
        state_ref[hd] = st
        head_vals[hd] = (o_intra, o_inter)

    def head_norm(hd):
        vs = slice(hd * HGRN_DV, (hd + 1) * HGRN_DV)
        o_intra, o_inter = head_vals.pop(hd)
        o_h = o_intra + jnp.concatenate(o_inter, axis=0)
        o_h = o_h * lax.rsqrt(jnp.mean(o_h * o_h, axis=-1, keepdims=True) + NORM_EPS)
        og_ref[:, vs] = (o_h * hng_ref[:, vs] * gs_ref[:, vs]).astype(BF16)

    def gmlp_group(g):
        cs = slice(g * gd, (g + 1) * gd)
        w_g = jnp.where(c_col <= c_row, ws_ref[g], 0.0).astype(BF16)
        bias = bst_ref[:, g:g + 1]
        lnv = ((gv_ref[:, cs] - mu) * rstd * lng_ref[:, cs] + lnb_ref[:, cs]).astype(BF16)
        for n in range(T // GMLP_CHUNK):
            rs = slice(n * GMLP_CHUNK, (n + 1) * GMLP_CHUNK)
            mixed = _dot(w_g, lnv[rs, :]) + bias
            a_ref[rs, cs] = (gu_ref[rs, cs] * mixed).astype(BF16)

    for it in range(HGRN_HEADS + 2):
        if it < HGRN_HEADS:
            head_scores(it)
        if 0 <= it - 1 < HGRN_HEADS:
            head_outputs(it - 1)
        if 0 <= it - 2 < HGRN_HEADS:
            head_norm(it - 2)
        if it < GMLP_GROUPS:
            gmlp_group(it)
        if it % 2 == 1 and gate_slabs:
            next_gate_slab()
    assert not gate_slabs and not head_vals

    for j in range(D // W):
        js = slice(j * W, (j + 1) * W)
        y_a = _dot(a_ref[...], wa_ref[:, js])
        y_b = _dot(og_ref[...], wb_ref[:, js])
        mg_ref[:, js] = (sga_ref[:, js] * y_a + sgb_ref[:, js] * y_b).astype(BF16)
    for j in range(D // W):
        js = slice(j * W, (j + 1) * W)
        o_ref[:, js] = x_ref[:, js] + _dot(mg_ref[...], wo_ref[:, js])


def _ffn_kernel(x_ref, nfg_ref, wgu_ref, wd_ref, nog_ref, o_ref, act_ref):
    d_ff = wd_ref.shape[0]
    x = x_ref[...]
    h = _rms_norm(x, nfg_ref[...]).astype(BF16)
    for j in range(d_ff // FFN_COL_CHUNK):
        cs = slice(j * FFN_COL_CHUNK, (j + 1) * FFN_COL_CHUNK)
        gate = _dot(h, wgu_ref[:, cs])
        up = _dot(h, wgu_ref[:, d_ff + j * FFN_COL_CHUNK:d_ff + (j + 1) * FFN_COL_CHUNK])
        act_ref[:, cs] = (gate * _sigmoid(gate) * up).astype(BF16)
    x2 = x + _dot(act_ref[...], wd_ref[...])
    o_ref[...] = _rms_norm(x2, nog_ref[...])


def _resident(shape):
    n = len(shape)
    return pl.BlockSpec(shape, lambda *_: (0,) * n, pipeline_mode=pl.Buffered(1))


def _mixer_call(x, nmg, w_in, lng, lnb, w_s, bst, lbt, hng, w_a, w_b, w_o):
    B, S, D = x.shape
    T = MIXER_SEQ_TILE
    seg = w_in.shape[1] // N_IN_SEGMENTS
    x_spec = pl.BlockSpec((None, T, D), lambda b, s: (b, s, 0))
    params = (nmg, w_in, lng, lnb, w_s, bst, lbt, hng, w_a, w_b, w_o)
    return pl.pallas_call(
        _mixer_kernel,
        grid=(B, S // T),
        in_specs=[x_spec] + [_resident(p.shape) for p in params],
        out_specs=x_spec,
        out_shape=jax.ShapeDtypeStruct(x.shape, x.dtype),
        scratch_shapes=[
            pltpu.VMEM((HGRN_HEADS, HGRN_DV, HGRN_DK), F32),
            pltpu.VMEM((T, seg), BF16),
            pltpu.VMEM((T, seg), BF16),
            pltpu.VMEM((T, seg), BF16),
            pltpu.VMEM((T, seg), BF16),
            pltpu.VMEM((T, seg), BF16),
            pltpu.VMEM((T // HGRN_CHUNK, seg), F32),
            pltpu.VMEM((T, seg), F32),
            pltpu.VMEM((T, seg), F32),
            pltpu.VMEM((T, seg), F32),
            pltpu.VMEM((T, D), F32),
            pltpu.VMEM((T, D), F32),
            pltpu.VMEM((T, seg), BF16),
            pltpu.VMEM((T, seg), BF16),
            pltpu.VMEM((T, D), BF16),
        ],
        compiler_params=pltpu.CompilerParams(
            dimension_semantics=("arbitrary", "arbitrary"),
            vmem_limit_bytes=V7X_VMEM_LIMIT_BYTES),
        name="mixer",
    )(x, *params)


def _ffn_call(x, nfg, w_gu, w_d, nog):
    N, D = x.shape
    T = FFN_TOKEN_TILE
    x_spec = pl.BlockSpec((T, D), lambda i: (i, 0))
    params = (nfg, w_gu, w_d, nog)
    return pl.pallas_call(
        _ffn_kernel,
        grid=(N // T,),
        in_specs=[x_spec] + [_resident(p.shape) for p in params],
        out_specs=x_spec,
        out_shape=jax.ShapeDtypeStruct(x.shape, x.dtype),
        scratch_shapes=[pltpu.VMEM((T, w_d.shape[0]), BF16)],
        compiler_params=pltpu.CompilerParams(
            dimension_semantics=("arbitrary",),
            vmem_limit_bytes=V7X_VMEM_LIMIT_BYTES),
        name="ffn",
    )(x, *params)


@jax.jit
def kernel(x, norm_mix_g, w_in, gmlp_ln_g, gmlp_ln_b, gmlp_w_s, gmlp_b_s, hgrn_lb_table,
           hgrn_norm_g, w_branch_a, w_branch_b, w_out, norm_ffn_g, w_gate_up, w_down,
           norm_final_g):
    B, S, D = x.shape
    depth = w_in.shape[0]
    assert depth == 1 and hgrn_lb_table.shape[0] == depth + 1
    assert S % MIXER_SEQ_TILE == 0 and (B * S) % FFN_TOKEN_TILE == 0
    assert w_down.shape[1] % FFN_COL_CHUNK == 0 and D % MIXER_COL_SLAB == 0
    l = 0
    x1 = _mixer_call(
        x, norm_mix_g[l][None, :], w_in[l].astype(BF16),
        gmlp_ln_g[l][None, :], gmlp_ln_b[l][None, :], gmlp_w_s[l], gmlp_b_s[l].T,
        hgrn_lb_table, hgrn_norm_g[l][None, :],
        w_branch_a[l].astype(BF16), w_branch_b[l].astype(BF16), w_out[l].astype(BF16))
    out = _ffn_call(
        x1.reshape(B * S, D), norm_ffn_g[l][None, :], w_gate_up[l].astype(BF16),
        w_down[l].astype(BF16), norm_final_g[None, :])
    return out.reshape(B, S, D)
```

```python
import math

import jax
import jax.numpy as jnp
from jax import lax
from jax.experimental import pallas as pl
from jax.experimental.pallas import tpu as pltpu

F32 = jnp.float32
BF16 = jnp.bfloat16

NORM_EPS = 1e-6
GMLP_GROUPS = 8
GMLP_CHUNK = 128
HGRN_HEADS = 8
HGRN_DK = 128
HGRN_DV = 128
HGRN_CHUNK = 64
HGRN_SCALE = HGRN_DK ** -0.5
N_IN_SEGMENTS = 8

MIXER_SEQ_TILE = 256
FFN_TOKEN_TILE = 512
FFN_COL_CHUNK = 256
PACK_ROW_BLOCK = 256
MIXER_COL_SLAB = 512
V7X_VMEM_LIMIT_BYTES = 56 * 1024 * 1024


def _sigmoid(x):
    return 0.5 * (1.0 + jnp.tanh(0.5 * x))


def _gelu_tanh(x):
    c = math.sqrt(2.0 / math.pi)
    return 0.5 * x * (1.0 + jnp.tanh(c * (x + 0.044715 * (x * x * x))))


def _rms_norm(x, gain):
    return x * lax.rsqrt(jnp.mean(x * x, axis=-1, keepdims=True) + NORM_EPS) * gain


def _dot(a, b):
    return jnp.dot(a, b, preferred_element_type=F32)


def _dot_nt(a, b):
    return lax.dot_general(a, b, (((1,), (1,)), ((), ())), preferred_element_type=F32)


def _dot_tn(a, b):
    return lax.dot_general(a, b, (((0,), (0,)), ((), ())), preferred_element_type=F32)


def _unpack_rows(packed):
    return pltpu.bitcast(packed, BF16)


def _pack_kernel(*refs):
    n = len(refs) // 2
    for w_ref, o_ref in zip(refs[:n], refs[n:]):
        o_ref[...] = pltpu.bitcast(w_ref[...].astype(BF16), jnp.uint32)


def _mixer_kernel(x_ref, nmg_ref, win_ref, lng_ref, lnb_ref, ws_ref, bst_ref, lbt_ref,
                  hng_ref, wa_ref, wb_ref, wo_ref, o_ref,
                  state_ref, qin_ref, kin_ref, qdec_ref, kdec_ref, v_ref, decay_ref, gs_ref,
                  gu_ref, gv_ref, sga_ref, sgb_ref, a_ref, og_ref, mg_ref):
    T, D = x_ref.shape
    seg = win_ref.shape[1] // N_IN_SEGMENTS
    W = MIXER_COL_SLAB
    n_slabs = seg // W
    C = HGRN_CHUNK
    n_chunks = T // C

    @pl.when(pl.program_id(1) == 0)
    def _():
        state_ref[...] = jnp.zeros_like(state_ref)

    h = _rms_norm(x_ref[...], nmg_ref[...]).astype(BF16)

    def proj(i, j):
        c0 = i * seg + j * W
        return _dot(h, _unpack_rows(win_ref[:, c0:c0 + W]))

    row = lax.broadcasted_iota(jnp.int32, (T, T), 0)
    col = lax.broadcasted_iota(jnp.int32, (T, T), 1)
    causal = jnp.logical_and((row >> 6) == (col >> 6), col <= row)
    tri = jnp.where(causal, 1.0, 0.0).astype(BF16)

    lbt = lbt_ref[...]
    lbe = jnp.exp(lbt - jnp.max(lbt, axis=0, keepdims=True))
    lb = lbe[0:1, :] / jnp.sum(lbe, axis=0, keepdims=True)

    gv_sum = jnp.zeros((T, 1), F32)
    f_logits = [proj(3, j) for j in range(n_slabs)]
    gate_vals = []
    for j in range(n_slabs):
        lb_j = lb[:, j * W:(j + 1) * W]
        f = lb_j + (1.0 - lb_j) * _sigmoid(f_logits[j])
        log_f = jnp.log(f)
        lf_hi = log_f.astype(BF16)
        lf_lo = (log_f - lf_hi.astype(F32)).astype(BF16)
        gate_vals.append((1.0 - f, _dot(tri, lf_hi) + _dot(tri, lf_lo), proj(2, j) * HGRN_SCALE))
    for j in range(n_slabs):
        js = slice(j * W, (j + 1) * W)
        kk, A, q = gate_vals[j]
        a_mid_rows, e_mid_rows, e_lm_rows = [], [], []
        for c in range(n_chunks):
            a_mid = A[c * C + C // 2 - 1:c * C + C // 2, :]
            a_last = A[c * C + C - 1:c * C + C, :]
            a_mid_rows.append(jnp.broadcast_to(a_mid, (C, W)))
            e_mid_rows.append(jnp.broadcast_to(jnp.exp(a_mid), (C, W)))
            e_lm_rows.append(jnp.broadcast_to(jnp.exp(a_last - a_mid), (C, W)))
            decay_ref[c:c + 1, js] = jnp.exp(a_last)
        rel = A - jnp.concatenate(a_mid_rows, axis=0)
        q_in = q * jnp.exp(rel)
        k_in = kk * jnp.exp(-rel)
        qdec_ref[:, js] = (q_in * jnp.concatenate(e_mid_rows, axis=0)).astype(BF16)
        kdec_ref[:, js] = (k_in * jnp.concatenate(e_lm_rows, axis=0)).astype(BF16)
        qin_ref[:, js] = q_in.astype(BF16)
        kin_ref[:, js] = k_in.astype(BF16)
        v_ref[:, js] = proj(4, j).astype(BF16)
        g_out = proj(5, j)
        gs_ref[:, js] = g_out * _sigmoid(g_out)
        gu_ref[:, js] = _gelu_tanh(proj(0, j))
        gv = _gelu_tanh(proj(1, j))
        gv_ref[:, js] = gv
        gv_sum = gv_sum + jnp.sum(gv, axis=-1, keepdims=True)

    gate_slabs = [(sga_ref, 6, j) for j in range(n_slabs)] + [(sgb_ref, 7, j) for j in range(n_slabs)]

    def next_gate_slab():
        ref, i, j = gate_slabs.pop(0)
        ref[:, j * W:(j + 1) * W] = _sigmoid(proj(i, j))

    next_gate_slab()
    mu = gv_sum * (1.0 / seg)
    sq_sum = jnp.zeros((T, 1), F32)
    for j in range(n_slabs):
        cen = gv_ref[:, j * W:(j + 1) * W] - mu
        sq_sum = sq_sum + jnp.sum(cen * cen, axis=-1, keepdims=True)
    rstd = lax.rsqrt(sq_sum * (1.0 / seg) + NORM_EPS)
    c_row = lax.broadcasted_iota(jnp.int32, (GMLP_CHUNK, GMLP_CHUNK), 0)
    c_col = lax.broadcasted_iota(jnp.int32, (GMLP_CHUNK, GMLP_CHUNK), 1)
    gd = seg // GMLP_GROUPS

    head_vals = {}

    def head_scores(hd):
        ks = slice(hd * HGRN_DK, (hd + 1) * HGRN_DK)
        vs = slice(hd * HGRN_DV, (hd + 1) * HGRN_DV)
        raw = _dot_nt(qin_ref[:, ks], kin_ref[:, ks])
        d_state = [_dot_tn(v_ref[c * C:(c + 1) * C, vs], kdec_ref[c * C:(c + 1) * C, ks])
                   for c in range(n_chunks)]
        head_vals[hd] = (raw, d_state)

    def head_outputs(hd):
        ks = slice(hd * HGRN_DK, (hd + 1) * HGRN_DK)
        vs = slice(hd * HGRN_DV, (hd + 1) * HGRN_DV)
        raw, d_state = head_vals[hd]
        scores = jnp.where(causal, raw, 0.0).astype(BF16)
        o_intra = _dot(scores, v_ref[:, vs])
        st = state_ref[hd]
        o_inter = []
        for c in range(n_chunks):
            o_inter.append(_dot_nt(qdec_ref[c * C:(c + 1) * C, ks], st.astype(BF16)))
            st = st * decay_ref[c:c + 1, ks] + d_state[c]
        state_ref[hd] = st
        head_vals[hd] = (o_intra, o_inter)

    def head_norm(hd):
        vs = slice(hd * HGRN_DV, (hd + 1) * HGRN_DV)
        o_intra, o_inter = head_vals.pop(hd)
        o_h = o_intra + jnp.concatenate(o_inter, axis=0)
        o_h = o_h * lax.rsqrt(jnp.mean(o_h * o_h, axis=-1, keepdims=True) + NORM_EPS)
        og_ref[:, vs] = (o_h * hng_ref[:, vs] * gs_ref[:, vs]).astype(BF16)

    def gmlp_group(g):
        cs = slice(g * gd, (g + 1) * gd)
        w_g = jnp.where(c_col <= c_row, ws_ref[g], 0.0).astype(BF16)
        bias = bst_ref[:, g:g + 1]
        lnv = ((gv_ref[:, cs] - mu) * rstd * lng_ref[:, cs] + lnb_ref[:, cs]).astype(BF16)
        for n in range(T // GMLP_CHUNK):
            rs = slice(n * GMLP_CHUNK, (n + 1) * GMLP_CHUNK)
            mixed = _dot(w_g, lnv[rs, :]) + bias
            a_ref[rs, cs] = (gu_ref[rs, cs] * mixed).astype(BF16)

    for it in range(HGRN_HEADS + 2):
        if it < HGRN_HEADS:
            head_scores(it)
        if 0 <= it - 1 < HGRN_HEADS:
            head_outputs(it - 1)
        if 0 <= it - 2 < HGRN_HEADS:
            head_norm(it - 2)
        if it < GMLP_GROUPS:
            gmlp_group(it)
        if it % 2 == 1 and gate_slabs:
            next_gate_slab()
    assert not gate_slabs and not head_vals

    for j in range(D // W):
        js = slice(j * W, (j + 1) * W)
        y_a = _dot(a_ref[...], _unpack_rows(wa_ref[:, js]))
        y_b = _dot(og_ref[...], _unpack_rows(wb_ref[:, js]))
        mg_ref[:, js] = (sga_ref[:, js] * y_a + sgb_ref[:, js] * y_b).astype(BF16)
    for j in range(D // W):
        js = slice(j * W, (j + 1) * W)
        o_ref[:, js] = x_ref[:, js] + _dot(mg_ref[...], _unpack_rows(wo_ref[:, js]))


def _ffn_kernel(x_ref, nfg_ref, wgu_ref, wd_ref, nog_ref, o_ref, act_ref):
    d_ff = wd_ref.shape[0]
    x = x_ref[...]
    h = _rms_norm(x, nfg_ref[...]).astype(BF16)
    for j in range(d_ff // FFN_COL_CHUNK):
        cs = slice(j * FFN_COL_CHUNK, (j + 1) * FFN_COL_CHUNK)
        gate = _dot(h, wgu_ref[:, cs])
        up = _dot(h, wgu_ref[:, d_ff + j * FFN_COL_CHUNK:d_ff + (j + 1) * FFN_COL_CHUNK])
        act_ref[:, cs] = (gate * _sigmoid(gate) * up).astype(BF16)
    x2 = x + _dot(act_ref[...], wd_ref[...])
    o_ref[...] = _rms_norm(x2, nog_ref[...])


def _pack_call(*weights):
    k = weights[0].shape[0]
    rows = PACK_ROW_BLOCK
    assert all(w.shape[0] == k for w in weights) and k % rows == 0
    return pl.pallas_call(
        _pack_kernel,
        grid=(k // rows,),
        in_specs=[pl.BlockSpec((rows, w.shape[1]), lambda i: (i, 0)) for w in weights],
        out_specs=[pl.BlockSpec((rows // 2, w.shape[1]), lambda i: (i, 0)) for w in weights],
        out_shape=[jax.ShapeDtypeStruct((k // 2, w.shape[1]), jnp.uint32) for w in weights],
        compiler_params=pltpu.CompilerParams(
            dimension_semantics=("arbitrary",),
            vmem_limit_bytes=V7X_VMEM_LIMIT_BYTES),
        name="pack_weights",
    )(*weights)


def _resident(shape):
    n = len(shape)
    return pl.BlockSpec(shape, lambda *_: (0,) * n, pipeline_mode=pl.Buffered(1))


def _mixer_call(x, nmg, w_in, lng, lnb, w_s, bst, lbt, hng, w_a, w_b, w_o):
    B, S, D = x.shape
    T = MIXER_SEQ_TILE
    seg = w_in.shape[1] // N_IN_SEGMENTS
    x_spec = pl.BlockSpec((None, T, D), lambda b, s: (b, s, 0))
    params = (nmg, w_in, lng, lnb, w_s, bst, lbt, hng, w_a, w_b, w_o)
    return pl.pallas_call(
        _mixer_kernel,
        grid=(B, S // T),
        in_specs=[x_spec] + [_resident(p.shape) for p in params],
        out_specs=x_spec,
        out_shape=jax.ShapeDtypeStruct(x.shape, x.dtype),
        scratch_shapes=[
            pltpu.VMEM((HGRN_HEADS, HGRN_DV, HGRN_DK), F32),
            pltpu.VMEM((T, seg), BF16),
            pltpu.VMEM((T, seg), BF16),
            pltpu.VMEM((T, seg), BF16),
            pltpu.VMEM((T, seg), BF16),
            pltpu.VMEM((T, seg), BF16),
            pltpu.VMEM((T // HGRN_CHUNK, seg), F32),
            pltpu.VMEM((T, seg), F32),
            pltpu.VMEM((T, seg), F32),
            pltpu.VMEM((T, seg), F32),
            pltpu.VMEM((T, D), F32),
            pltpu.VMEM((T, D), F32),
            pltpu.VMEM((T, seg), BF16),
            pltpu.VMEM((T, seg), BF16),
            pltpu.VMEM((T, D), BF16),
        ],
        compiler_params=pltpu.CompilerParams(
            dimension_semantics=("arbitrary", "arbitrary"),
            vmem_limit_bytes=V7X_VMEM_LIMIT_BYTES),
        name="mixer",
    )(x, *params)


def _ffn_call(x, nfg, w_gu, w_d, nog):
    N, D = x.shape
    T = FFN_TOKEN_TILE
    x_spec = pl.BlockSpec((T, D), lambda i: (i, 0))
    params = (nfg, w_gu, w_d, nog)
    return pl.pallas_call(
        _ffn_kernel,
        grid=(N // T,),
        in_specs=[x_spec] + [_resident(p.shape) for p in params],
        out_specs=x_spec,
        out_shape=jax.ShapeDtypeStruct(x.shape, x.dtype),
        scratch_shapes=[pltpu.VMEM((T, w_d.shape[0]), BF16)],
        compiler_params=pltpu.CompilerParams(
            dimension_semantics=("arbitrary",),
            vmem_limit_bytes=V7X_VMEM_LIMIT_BYTES),
        name="ffn",
    )(x, *params)


@jax.jit
def kernel(x, norm_mix_g, w_in, gmlp_ln_g, gmlp_ln_b, gmlp_w_s, gmlp_b_s, hgrn_lb_table,
           hgrn_norm_g, w_branch_a, w_branch_b, w_out, norm_ffn_g, w_gate_up, w_down,
           norm_final_g):
    B, S, D = x.shape
    depth = w_in.shape[0]
    assert depth == 1 and hgrn_lb_table.shape[0] == depth + 1
    assert S % MIXER_SEQ_TILE == 0 and (B * S) % FFN_TOKEN_TILE == 0
    assert w_down.shape[1] % FFN_COL_CHUNK == 0 and D % MIXER_COL_SLAB == 0
    l = 0
    w_in_p, w_a_p, w_b_p, w_o_p = _pack_call(w_in[l], w_branch_a[l], w_branch_b[l], w_out[l])
    x1 = _mixer_call(
        x, norm_mix_g[l][None, :], w_in_p,
        gmlp_ln_g[l][None, :], gmlp_ln_b[l][None, :], gmlp_w_s[l], gmlp_b_s[l].T,
        hgrn_lb_table, hgrn_norm_g[l][None, :], w_a_p, w_b_p, w_o_p)
    out = _ffn_call(
        x1.reshape(B * S, D), norm_ffn_g[l][None, :], w_gate_up[l].astype(BF16),
        w_down[l].astype(BF16), norm_final_g[None, :])
    return out.reshape(B, S, D)
```

```python
import math

import jax
import jax.numpy as jnp
from jax import lax
from jax.experimental import pallas as pl
from jax.experimental.pallas import tpu as pltpu

F32 = jnp.float32
BF16 = jnp.bfloat16

NORM_EPS = 1e-6
GMLP_GROUPS = 8
GMLP_CHUNK = 128
HGRN_HEADS = 8
HGRN_DK = 128
HGRN_DV = 128
HGRN_CHUNK = 64
HGRN_SCALE = HGRN_DK ** -0.5
N_IN_SEGMENTS = 8

MIXER_SEQ_TILE = 256
FFN_TOKEN_TILE = 512
FFN_COL_CHUNK = 256
FFN_CARRIED_CHUNKS = 2
PACK_ROW_BLOCK = 256
MIXER_COL_SLAB = 512
V7X_VMEM_LIMIT_BYTES = 56 * 1024 * 1024


def _sigmoid(x):
    return 0.5 * (1.0 + jnp.tanh(0.5 * x))


def _gelu_tanh(x):
    c = math.sqrt(2.0 / math.pi)
    return 0.5 * x * (1.0 + jnp.tanh(c * (x + 0.044715 * (x * x * x))))


def _rms_norm(x, gain):
    return x * lax.rsqrt(jnp.mean(x * x, axis=-1, keepdims=True) + NORM_EPS) * gain


def _dot(a, b):
    return jnp.dot(a, b, preferred_element_type=F32)


def _dot_nt(a, b):
    return lax.dot_general(a, b, (((1,), (1,)), ((), ())), preferred_element_type=F32)


def _dot_tn(a, b):
    return lax.dot_general(a, b, (((0,), (0,)), ((), ())), preferred_element_type=F32)


def _unpack_rows(packed):
    return pltpu.bitcast(packed, BF16)


def _pack_kernel(*refs):
    n = len(refs) // 2
    for w_ref, o_ref in zip(refs[:n], refs[n:]):
        o_ref[...] = pltpu.bitcast(w_ref[...].astype(BF16), jnp.uint32)


def _mixer_kernel(x_ref, nmg_ref, win_ref, lng_ref, lnb_ref, ws_ref, bst_ref, lbt_ref,
                  hng_ref, wa_ref, wb_ref, wo_ref, o_ref,
                  state_ref, qin_ref, kin_ref, qdec_ref, kdec_ref, v_ref, decay_ref, gs_ref,
                  gu_ref, gv_ref, sga_ref, sgb_ref, a_ref, og_ref, mg_ref):
    T, D = x_ref.shape
    seg = win_ref.shape[1] // N_IN_SEGMENTS
    W = MIXER_COL_SLAB
    n_slabs = seg // W
    C = HGRN_CHUNK
    n_chunks = T // C

    @pl.when(pl.program_id(1) == 0)
    def _():
        state_ref[...] = jnp.zeros_like(state_ref)

    h = _rms_norm(x_ref[...], nmg_ref[...]).astype(BF16)

    def proj(i, j):
        c0 = i * seg + j * W
        return _dot(h, _unpack_rows(win_ref[:, c0:c0 + W]))

    row = lax.broadcasted_iota(jnp.int32, (T, T), 0)
    col = lax.broadcasted_iota(jnp.int32, (T, T), 1)
    causal = jnp.logical_and((row >> 6) == (col >> 6), col <= row)
    tri = jnp.where(causal, 1.0, 0.0).astype(BF16)

    lbt = lbt_ref[...]
    lbe = jnp.exp(lbt - jnp.max(lbt, axis=0, keepdims=True))
    lb = lbe[0:1, :] / jnp.sum(lbe, axis=0, keepdims=True)

    gv_sum = jnp.zeros((T, 1), F32)
    f_logits = [proj(3, j) for j in range(n_slabs)]
    gate_vals = []
    for j in range(n_slabs):
        lb_j = lb[:, j * W:(j + 1) * W]
        f = lb_j + (1.0 - lb_j) * _sigmoid(f_logits[j])
        log_f = jnp.log(f)
        lf_hi = log_f.astype(BF16)
        lf_lo = (log_f - lf_hi.astype(F32)).astype(BF16)
        gate_vals.append((1.0 - f, _dot(tri, lf_hi) + _dot(tri, lf_lo), proj(2, j) * HGRN_SCALE))
    for j in range(n_slabs):
        js = slice(j * W, (j + 1) * W)
        kk, A, q = gate_vals[j]
        a_mid_rows, e_mid_rows, e_lm_rows = [], [], []
        for c in range(n_chunks):
            a_mid = A[c * C + C // 2 - 1:c * C + C // 2, :]
            a_last = A[c * C + C - 1:c * C + C, :]
            a_mid_rows.append(jnp.broadcast_to(a_mid, (C, W)))
            e_mid_rows.append(jnp.broadcast_to(jnp.exp(a_mid), (C, W)))
            e_lm_rows.append(jnp.broadcast_to(jnp.exp(a_last - a_mid), (C, W)))
            decay_ref[c:c + 1, js] = jnp.exp(a_last)
        rel = A - jnp.concatenate(a_mid_rows, axis=0)
        q_in = q * jnp.exp(rel)
        k_in = kk * jnp.exp(-rel)
        qdec_ref[:, js] = (q_in * jnp.concatenate(e_mid_rows, axis=0)).astype(BF16)
        kdec_ref[:, js] = (k_in * jnp.concatenate(e_lm_rows, axis=0)).astype(BF16)
        qin_ref[:, js] = q_in.astype(BF16)
        kin_ref[:, js] = k_in.astype(BF16)
        v_ref[:, js] = proj(4, j).astype(BF16)
        g_out = proj(5, j)
        gs_ref[:, js] = g_out * _sigmoid(g_out)
        gu_ref[:, js] = _gelu_tanh(proj(0, j))
        gv = _gelu_tanh(proj(1, j))
        gv_ref[:, js] = gv
        gv_sum = gv_sum + jnp.sum(gv, axis=-1, keepdims=True)

    gate_slabs = [(sga_ref, 6, j) for j in range(n_slabs)] + [(sgb_ref, 7, j) for j in range(n_slabs)]

    def next_gate_slab():
        ref, i, j = gate_slabs.pop(0)
        ref[:, j * W:(j + 1) * W] = _sigmoid(proj(i, j))

    next_gate_slab()
    mu = gv_sum * (1.0 / seg)
    sq_sum = jnp.zeros((T, 1), F32)
    for j in range(n_slabs):
        cen = gv_ref[:, j * W:(j + 1) * W] - mu
        sq_sum = sq_sum + jnp.sum(cen * cen, axis=-1, keepdims=True)
    rstd = lax.rsqrt(sq_sum * (1.0 / seg) + NORM_EPS)
    c_row = lax.broadcasted_iota(jnp.int32, (GMLP_CHUNK, GMLP_CHUNK), 0)
    c_col = lax.broadcasted_iota(jnp.int32, (GMLP_CHUNK, GMLP_CHUNK), 1)
    gd = seg // GMLP_GROUPS

    head_vals = {}

    def head_scores(hd):
        ks = slice(hd * HGRN_DK, (hd + 1) * HGRN_DK)
        vs = slice(hd * HGRN_DV, (hd + 1) * HGRN_DV)
        raw = _dot_nt(qin_ref[:, ks], kin_ref[:, ks])
        d_state = [_dot_tn(v_ref[c * C:(c + 1) * C, vs], kdec_ref[c * C:(c + 1) * C, ks])
                   for c in range(n_chunks)]
        head_vals[hd] = (raw, d_state)

    def head_outputs(hd):
        ks = slice(hd * HGRN_DK, (hd + 1) * HGRN_DK)
        vs = slice(hd * HGRN_DV, (hd + 1) * HGRN_DV)
        raw, d_state = head_vals[hd]
        scores = jnp.where(causal, raw, 0.0).astype(BF16)
        o_intra = _dot(scores, v_ref[:, vs])
        st = state_ref[hd]
        o_inter = []
        for c in range(n_chunks):
            o_inter.append(_dot_nt(qdec_ref[c * C:(c + 1) * C, ks], st.astype(BF16)))
            st = st * decay_ref[c:c + 1, ks] + d_state[c]
        state_ref[hd] = st
        head_vals[hd] = (o_intra, o_inter)

    def head_norm(hd):
        vs = slice(hd * HGRN_DV, (hd + 1) * HGRN_DV)
        o_intra, o_inter = head_vals.pop(hd)
        o_h = o_intra + jnp.concatenate(o_inter, axis=0)
        o_h = o_h * lax.rsqrt(jnp.mean(o_h * o_h, axis=-1, keepdims=True) + NORM_EPS)
        og_ref[:, vs] = (o_h * hng_ref[:, vs] * gs_ref[:, vs]).astype(BF16)

    def gmlp_group(g):
        cs = slice(g * gd, (g + 1) * gd)
        w_g = jnp.where(c_col <= c_row, ws_ref[g], 0.0).astype(BF16)
        bias = bst_ref[:, g:g + 1]
        lnv = ((gv_ref[:, cs] - mu) * rstd * lng_ref[:, cs] + lnb_ref[:, cs]).astype(BF16)
        for n in range(T // GMLP_CHUNK):
            rs = slice(n * GMLP_CHUNK, (n + 1) * GMLP_CHUNK)
            mixed = _dot(w_g, lnv[rs, :]) + bias
            a_ref[rs, cs] = (gu_ref[rs, cs] * mixed).astype(BF16)

    for it in range(HGRN_HEADS + 2):
        if it < HGRN_HEADS:
            head_scores(it)
        if 0 <= it - 1 < HGRN_HEADS:
            head_outputs(it - 1)
        if 0 <= it - 2 < HGRN_HEADS:
            head_norm(it - 2)
        if it < GMLP_GROUPS:
            gmlp_group(it)
        if it % 2 == 1 and gate_slabs:
            next_gate_slab()
    assert not gate_slabs and not head_vals

    for j in range(D // W):
        js = slice(j * W, (j + 1) * W)
        y_a = _dot(a_ref[...], _unpack_rows(wa_ref[:, js]))
        y_b = _dot(og_ref[...], _unpack_rows(wb_ref[:, js]))
        mg_ref[:, js] = (sga_ref[:, js] * y_a + sgb_ref[:, js] * y_b).astype(BF16)
    for j in range(D // W):
        js = slice(j * W, (j + 1) * W)
        o_ref[:, js] = x_ref[:, js] + _dot(mg_ref[...], _unpack_rows(wo_ref[:, js]))


def _ffn_kernel(x_ref, xn_ref, nfg_ref, wgu_ref, wd_ref, nog_ref, o_ref, h_ref, act_ref, carry_ref):
    d_ff = wd_ref.shape[0]
    Wc = FFN_COL_CHUNK
    n_carry = FFN_CARRIED_CHUNKS

    read_slot = pl.program_id(0) % 2
    write_slot = 1 - read_slot

    def act_chunk(j):
        gate = _dot(h_ref[...], wgu_ref[:, j * Wc:(j + 1) * Wc])
        up = _dot(h_ref[...], wgu_ref[:, d_ff + j * Wc:d_ff + (j + 1) * Wc])
        return (gate * _sigmoid(gate) * up).astype(BF16)

    @pl.when(pl.program_id(0) == 0)
    def _():
        h_ref[...] = _rms_norm(x_ref[...], nfg_ref[...]).astype(BF16)
        for j in range(n_carry):
            carry_ref[0, :, j * Wc:(j + 1) * Wc] = act_chunk(j)

    for j in range(n_carry, d_ff // Wc):
        act_ref[:, (j - n_carry) * Wc:(j - n_carry + 1) * Wc] = act_chunk(j)
    h_ref[...] = _rms_norm(xn_ref[...], nfg_ref[...]).astype(BF16)
    part = wd_ref.shape[1] // n_carry
    y_parts = []
    for j in range(n_carry):
        act = jnp.concatenate([carry_ref[read_slot], act_ref[...]], axis=1)
        y_parts.append(_dot(act, wd_ref[:, j * part:(j + 1) * part]))
        carry_ref[write_slot, :, j * Wc:(j + 1) * Wc] = act_chunk(j)
    x2 = x_ref[...] + jnp.concatenate(y_parts, axis=1)
    o_ref[...] = _rms_norm(x2, nog_ref[...])


def _pack_call(*weights):
    k = weights[0].shape[0]
    rows = PACK_ROW_BLOCK
    assert all(w.shape[0] == k for w in weights) and k % rows == 0
    return pl.pallas_call(
        _pack_kernel,
        grid=(k // rows,),
        in_specs=[pl.BlockSpec((rows, w.shape[1]), lambda i: (i, 0)) for w in weights],
        out_specs=[pl.BlockSpec((rows // 2, w.shape[1]), lambda i: (i, 0)) for w in weights],
        out_shape=[jax.ShapeDtypeStruct((k // 2, w.shape[1]), jnp.uint32) for w in weights],
        compiler_params=pltpu.CompilerParams(
            dimension_semantics=("arbitrary",),
            vmem_limit_bytes=V7X_VMEM_LIMIT_BYTES),
        name="pack_weights",
    )(*weights)


def _resident(shape):
    n = len(shape)
    return pl.BlockSpec(shape, lambda *_: (0,) * n, pipeline_mode=pl.Buffered(1))


def _mixer_call(x, nmg, w_in, lng, lnb, w_s, bst, lbt, hng, w_a, w_b, w_o):
    B, S, D = x.shape
    T = MIXER_SEQ_TILE
    seg = w_in.shape[1] // N_IN_SEGMENTS
    x_spec = pl.BlockSpec((None, T, D), lambda b, s: (b, s, 0))
    params = (nmg, w_in, lng, lnb, w_s, bst, lbt, hng, w_a, w_b, w_o)
    return pl.pallas_call(
        _mixer_kernel,
        grid=(B, S // T),
        in_specs=[x_spec] + [_resident(p.shape) for p in params],
        out_specs=x_spec,
        out_shape=jax.ShapeDtypeStruct(x.shape, x.dtype),
        scratch_shapes=[
            pltpu.VMEM((HGRN_HEADS, HGRN_DV, HGRN_DK), F32),
            pltpu.VMEM((T, seg), BF16),
            pltpu.VMEM((T, seg), BF16),
            pltpu.VMEM((T, seg), BF16),
            pltpu.VMEM((T, seg), BF16),
            pltpu.VMEM((T, seg), BF16),
            pltpu.VMEM((T // HGRN_CHUNK, seg), F32),
            pltpu.VMEM((T, seg), F32),
            pltpu.VMEM((T, seg), F32),
            pltpu.VMEM((T, seg), F32),
            pltpu.VMEM((T, D), F32),
            pltpu.VMEM((T, D), F32),
            pltpu.VMEM((T, seg), BF16),
            pltpu.VMEM((T, seg), BF16),
            pltpu.VMEM((T, D), BF16),
        ],
        compiler_params=pltpu.CompilerParams(
            dimension_semantics=("arbitrary", "arbitrary"),
            vmem_limit_bytes=V7X_VMEM_LIMIT_BYTES),
        name="mixer",
    )(x, *params)


def _ffn_call(x, nfg, w_gu, w_d, nog):
    N, D = x.shape
    T = FFN_TOKEN_TILE
    n_tiles = N // T
    n_carried = FFN_CARRIED_CHUNKS * FFN_COL_CHUNK
    x_spec = pl.BlockSpec((T, D), lambda i: (i, 0))
    x_next_spec = pl.BlockSpec((T, D), lambda i: (jnp.minimum(i + 1, n_tiles - 1), 0))
    params = (nfg, w_gu, w_d, nog)
    return pl.pallas_call(
        _ffn_kernel,
        grid=(n_tiles,),
        in_specs=[x_spec, x_next_spec] + [_resident(p.shape) for p in params],
        out_specs=x_spec,
        out_shape=jax.ShapeDtypeStruct(x.shape, x.dtype),
        scratch_shapes=[
            pltpu.VMEM((T, D), BF16),
            pltpu.VMEM((T, w_d.shape[0] - n_carried), BF16),
            pltpu.VMEM((2, T, n_carried), BF16),
        ],
        compiler_params=pltpu.CompilerParams(
            dimension_semantics=("arbitrary",),
            vmem_limit_bytes=V7X_VMEM_LIMIT_BYTES),
        name="ffn",
    )(x, x, *params)


@jax.jit
def kernel(x, norm_mix_g, w_in, gmlp_ln_g, gmlp_ln_b, gmlp_w_s, gmlp_b_s, hgrn_lb_table,
           hgrn_norm_g, w_branch_a, w_branch_b, w_out, norm_ffn_g, w_gate_up, w_down,
           norm_final_g):
    B, S, D = x.shape
    depth = w_in.shape[0]
    assert depth == 1 and hgrn_lb_table.shape[0] == depth + 1
    assert S % MIXER_SEQ_TILE == 0 and (B * S) % FFN_TOKEN_TILE == 0
    assert w_down.shape[1] % FFN_COL_CHUNK == 0 and D % MIXER_COL_SLAB == 0
    assert D % FFN_CARRIED_CHUNKS == 0
    l = 0
    w_in_p, w_a_p, w_b_p, w_o_p = _pack_call(w_in[l], w_branch_a[l], w_branch_b[l], w_out[l])
    x1 = _mixer_call(
        x, norm_mix_g[l][None, :], w_in_p,
        gmlp_ln_g[l][None, :], gmlp_ln_b[l][None, :], gmlp_w_s[l], gmlp_b_s[l].T,
        hgrn_lb_table, hgrn_norm_g[l][None, :], w_a_p, w_b_p, w_o_p)
    out = _ffn_call(
        x1.reshape(B * S, D), norm_ffn_g[l][None, :], w_gate_up[l].astype(BF16),
        w_down[l].astype(BF16), norm_final_g[None, :])
    return out.reshape(B, S, D)
```

```python
import math

import jax
import jax.numpy as jnp
from jax import lax
from jax.experimental import pallas as pl
from jax.experimental.pallas import tpu as pltpu

F32 = jnp.float32
BF16 = jnp.bfloat16

NORM_EPS = 1e-6
GMLP_GROUPS = 8
GMLP_CHUNK = 128
HGRN_HEADS = 8
HGRN_DK = 128
HGRN_DV = 128
HGRN_CHUNK = 64
HGRN_SCALE = HGRN_DK ** -0.5
N_IN_SEGMENTS = 8

MIXER_SEQ_TILE = 256
FFN_TOKEN_TILE = 512
FFN_COL_CHUNK = 256
FFN_CARRIED_CHUNKS = 2
PACK_ROW_BLOCK = 256
MIXER_COL_SLAB = 512
V7X_VMEM_LIMIT_BYTES = 56 * 1024 * 1024


def _silu(x):
    h = 0.5 * x
    return h + h * jnp.tanh(h)


def _gelu_tanh(x):
    c = math.sqrt(2.0 / math.pi)
    h = 0.5 * x
    return h + h * jnp.tanh(x * (c + (c * 0.044715) * (x * x)))


def _rms_norm(x, gain):
    return x * lax.rsqrt(jnp.mean(x * x, axis=-1, keepdims=True) + NORM_EPS) * gain


def _dot(a, b):
    return jnp.dot(a, b, preferred_element_type=F32)


def _dot_nt(a, b):
    return lax.dot_general(a, b, (((1,), (1,)), ((), ())), preferred_element_type=F32)


def _dot_tn(a, b):
    return lax.dot_general(a, b, (((0,), (0,)), ((), ())), preferred_element_type=F32)


def _unpack_rows(packed):
    return pltpu.bitcast(packed, BF16)


def _pack_kernel(*refs):
    n = len(refs) // 2
    for w_ref, o_ref in zip(refs[:n], refs[n:]):
        o_ref[...] = pltpu.bitcast(w_ref[...].astype(BF16), jnp.uint32)


def _mixer_kernel(x_ref, nmg_ref, win_ref, lng_ref, lnb_ref, ws_ref, bst_ref, lbt_ref,
                  hng_ref, wa_ref, wb_ref, wo_ref, o_ref,
                  state_ref, qin_ref, kin_ref, qdec_ref, kdec_ref, v_ref, decay_ref, gs_ref,
                  gu_ref, gv_ref, tga_ref, tgb_ref, a_ref, og_ref, mg_ref):
    T, D = x_ref.shape
    seg = win_ref.shape[1] // N_IN_SEGMENTS
    W = MIXER_COL_SLAB
    n_slabs = seg // W
    C = HGRN_CHUNK
    n_chunks = T // C

    @pl.when(pl.program_id(1) == 0)
    def _():
        state_ref[...] = jnp.zeros_like(state_ref)

    h = _rms_norm(x_ref[...], nmg_ref[...]).astype(BF16)

    def proj(i, j):
        c0 = i * seg + j * W
        return _dot(h, _unpack_rows(win_ref[:, c0:c0 + W]))

    row = lax.broadcasted_iota(jnp.int32, (T, T), 0)
    col = lax.broadcasted_iota(jnp.int32, (T, T), 1)
    causal = jnp.logical_and((row >> 6) == (col >> 6), col <= row)
    tri = jnp.where(causal, 1.0, 0.0).astype(BF16)

    lbt = lbt_ref[...]
    lbe = jnp.exp(lbt - jnp.max(lbt, axis=0, keepdims=True))
    lb = lbe[0:1, :] / jnp.sum(lbe, axis=0, keepdims=True)

    gv_sum = jnp.zeros((T, 1), F32)
    f_logits = [proj(3, j) for j in range(n_slabs)]
    gate_vals = []
    for j in range(n_slabs):
        lb_j = lb[:, j * W:(j + 1) * W]
        c1 = 0.5 * (1.0 - lb_j)
        f = (lb_j + c1) + c1 * jnp.tanh(0.5 * f_logits[j])
        log_f = jnp.log(f)
        lf_hi = log_f.astype(BF16)
        lf_lo = (log_f - lf_hi.astype(F32)).astype(BF16)
        gate_vals.append((1.0 - f, _dot(tri, lf_hi) + _dot(tri, lf_lo), proj(2, j) * HGRN_SCALE))
    for j in range(n_slabs):
        js = slice(j * W, (j + 1) * W)
        kk, A, q = gate_vals[j]
        a_mid_rows, e_mid_rows, e_lm_rows = [], [], []
        for c in range(n_chunks):
            a_mid = A[c * C + C // 2 - 1:c * C + C // 2, :]
            a_last = A[c * C + C - 1:c * C + C, :]
            a_mid_rows.append(jnp.broadcast_to(a_mid, (C, W)))
            e_mid_rows.append(jnp.broadcast_to(jnp.exp(a_mid), (C, W)))
            e_lm_rows.append(jnp.broadcast_to(jnp.exp(a_last - a_mid), (C, W)))
            decay_ref[c:c + 1, js] = jnp.exp(a_last)
        rel = A - jnp.concatenate(a_mid_rows, axis=0)
        q_in = q * jnp.exp(rel)
        k_in = kk * jnp.exp(-rel)
        qdec_ref[:, js] = (q_in * jnp.concatenate(e_mid_rows, axis=0)).astype(BF16)
        kdec_ref[:, js] = (k_in * jnp.concatenate(e_lm_rows, axis=0)).astype(BF16)
        qin_ref[:, js] = q_in.astype(BF16)
        kin_ref[:, js] = k_in.astype(BF16)
        v_ref[:, js] = proj(4, j).astype(BF16)
        gs_ref[:, js] = _silu(proj(5, j))
        gu_ref[:, js] = _gelu_tanh(proj(0, j))
        gv = _gelu_tanh(proj(1, j))
        gv_ref[:, js] = gv
        gv_sum = gv_sum + jnp.sum(gv, axis=-1, keepdims=True)

    gate_slabs = [(tga_ref, 6, j) for j in range(n_slabs)] + [(tgb_ref, 7, j) for j in range(n_slabs)]

    def next_gate_slab():
        ref, i, j = gate_slabs.pop(0)
        ref[:, j * W:(j + 1) * W] = jnp.tanh(0.5 * proj(i, j))

    next_gate_slab()
    mu = gv_sum * (1.0 / seg)
    sq_sum = jnp.zeros((T, 1), F32)
    for j in range(n_slabs):
        cen = gv_ref[:, j * W:(j + 1) * W] - mu
        sq_sum = sq_sum + jnp.sum(cen * cen, axis=-1, keepdims=True)
    rstd = lax.rsqrt(sq_sum * (1.0 / seg) + NORM_EPS)
    c_row = lax.broadcasted_iota(jnp.int32, (GMLP_CHUNK, GMLP_CHUNK), 0)
    c_col = lax.broadcasted_iota(jnp.int32, (GMLP_CHUNK, GMLP_CHUNK), 1)
    gd = seg // GMLP_GROUPS

    head_vals = {}

    def head_scores(hd):
        ks = slice(hd * HGRN_DK, (hd + 1) * HGRN_DK)
        vs = slice(hd * HGRN_DV, (hd + 1) * HGRN_DV)
        raw = _dot_nt(qin_ref[:, ks], kin_ref[:, ks])
        d_state = [_dot_tn(v_ref[c * C:(c + 1) * C, vs], kdec_ref[c * C:(c + 1) * C, ks])
                   for c in range(n_chunks)]
        head_vals[hd] = (raw, d_state)

    def head_outputs(hd):
        ks = slice(hd * HGRN_DK, (hd + 1) * HGRN_DK)
        vs = slice(hd * HGRN_DV, (hd + 1) * HGRN_DV)
        raw, d_state = head_vals[hd]
        scores = jnp.where(causal, raw, 0.0).astype(BF16)
        o_intra = _dot(scores, v_ref[:, vs])
        st = state_ref[hd]
        o_inter = []
        for c in range(n_chunks):
            o_inter.append(_dot_nt(qdec_ref[c * C:(c + 1) * C, ks], st.astype(BF16)))
            st = st * decay_ref[c:c + 1, ks] + d_state[c]
        state_ref[hd] = st
        head_vals[hd] = (o_intra, o_inter)

    def head_norm(hd):
        vs = slice(hd * HGRN_DV, (hd + 1) * HGRN_DV)
        o_intra, o_inter = head_vals.pop(hd)
        o_h = o_intra + jnp.concatenate(o_inter, axis=0)
        o_h = o_h * lax.rsqrt(jnp.mean(o_h * o_h, axis=-1, keepdims=True) + NORM_EPS)
        og_ref[:, vs] = (o_h * hng_ref[:, vs] * gs_ref[:, vs]).astype(BF16)

    def gmlp_group(g):
        cs = slice(g * gd, (g + 1) * gd)
        w_g = jnp.where(c_col <= c_row, ws_ref[g], 0.0).astype(BF16)
        bias = bst_ref[:, g:g + 1]
        lnv = ((gv_ref[:, cs] - mu) * rstd * lng_ref[:, cs] + lnb_ref[:, cs]).astype(BF16)
        for n in range(T // GMLP_CHUNK):
            rs = slice(n * GMLP_CHUNK, (n + 1) * GMLP_CHUNK)
            mixed = _dot(w_g, lnv[rs, :]) + bias
            a_ref[rs, cs] = (gu_ref[rs, cs] * mixed).astype(BF16)

    for it in range(HGRN_HEADS + 2):
        if it < HGRN_HEADS:
            head_scores(it)
        if 0 <= it - 1 < HGRN_HEADS:
            head_outputs(it - 1)
        if 0 <= it - 2 < HGRN_HEADS:
            head_norm(it - 2)
        if it < GMLP_GROUPS:
            gmlp_group(it)
        if it % 2 == 1 and gate_slabs:
            next_gate_slab()
    assert not gate_slabs and not head_vals

    for j in range(D // W):
        js = slice(j * W, (j + 1) * W)
        y_a = _dot(a_ref[...], _unpack_rows(wa_ref[:, js]))
        y_b = _dot(og_ref[...], _unpack_rows(wb_ref[:, js]))
        gated = (y_a + tga_ref[:, js] * y_a) + (y_b + tgb_ref[:, js] * y_b)
        mg_ref[:, js] = (0.5 * gated).astype(BF16)
    for j in range(D // W):
        js = slice(j * W, (j + 1) * W)
        o_ref[:, js] = x_ref[:, js] + _dot(mg_ref[...], _unpack_rows(wo_ref[:, js]))


def _ffn_kernel(x_ref, xn_ref, nfg_ref, wgu_ref, wd_ref, nog_ref, o_ref, h_ref, act_ref, carry_ref):
    d_ff = wd_ref.shape[0]
    Wc = FFN_COL_CHUNK
    n_carry = FFN_CARRIED_CHUNKS

    read_slot = pl.program_id(0) % 2
    write_slot = 1 - read_slot

    def act_chunk(j):
        gate = _dot(h_ref[...], wgu_ref[:, j * Wc:(j + 1) * Wc])
        up = _dot(h_ref[...], wgu_ref[:, d_ff + j * Wc:d_ff + (j + 1) * Wc])
        return (_silu(gate) * up).astype(BF16)

    @pl.when(pl.program_id(0) == 0)
    def _():
        h_ref[...] = _rms_norm(x_ref[...], nfg_ref[...]).astype(BF16)
        for j in range(n_carry):
            carry_ref[0, :, j * Wc:(j + 1) * Wc] = act_chunk(j)

    for j in range(n_carry, d_ff // Wc):
        act_ref[:, (j - n_carry) * Wc:(j - n_carry + 1) * Wc] = act_chunk(j)
    h_ref[...] = _rms_norm(xn_ref[...], nfg_ref[...]).astype(BF16)
    part = wd_ref.shape[1] // n_carry
    y_parts = []
    for j in range(n_carry):
        act = jnp.concatenate([carry_ref[read_slot], act_ref[...]], axis=1)
        y_parts.append(_dot(act, wd_ref[:, j * part:(j + 1) * part]))
        carry_ref[write_slot, :, j * Wc:(j + 1) * Wc] = act_chunk(j)
    x2 = x_ref[...] + jnp.concatenate(y_parts, axis=1)
    o_ref[...] = _rms_norm(x2, nog_ref[...])


def _pack_call(*weights):
    k = weights[0].shape[0]
    rows = PACK_ROW_BLOCK
    assert all(w.shape[0] == k for w in weights) and k % rows == 0
    return pl.pallas_call(
        _pack_kernel,
        grid=(k // rows,),
        in_specs=[pl.BlockSpec((rows, w.shape[1]), lambda i: (i, 0)) for w in weights],
        out_specs=[pl.BlockSpec((rows // 2, w.shape[1]), lambda i: (i, 0)) for w in weights],
        out_shape=[jax.ShapeDtypeStruct((k // 2, w.shape[1]), jnp.uint32) for w in weights],
        compiler_params=pltpu.CompilerParams(
            dimension_semantics=("arbitrary",),
            vmem_limit_bytes=V7X_VMEM_LIMIT_BYTES),
        name="pack_weights",
    )(*weights)


def _resident(shape):
    n = len(shape)
    return pl.BlockSpec(shape, lambda *_: (0,) * n, pipeline_mode=pl.Buffered(1))


def _mixer_call(x, nmg, w_in, lng, lnb, w_s, bst, lbt, hng, w_a, w_b, w_o):
    B, S, D = x.shape
    T = MIXER_SEQ_TILE
    seg = w_in.shape[1] // N_IN_SEGMENTS
    x_spec = pl.BlockSpec((None, T, D), lambda b, s: (b, s, 0))
    params = (nmg, w_in, lng, lnb, w_s, bst, lbt, hng, w_a, w_b, w_o)
    return pl.pallas_call(
        _mixer_kernel,
        grid=(B, S // T),
        in_specs=[x_spec] + [_resident(p.shape) for p in params],
        out_specs=x_spec,
        out_shape=jax.ShapeDtypeStruct(x.shape, x.dtype),
        scratch_shapes=[
            pltpu.VMEM((HGRN_HEADS, HGRN_DV, HGRN_DK), F32),
            pltpu.VMEM((T, seg), BF16),
            pltpu.VMEM((T, seg), BF16),
            pltpu.VMEM((T, seg), BF16),
            pltpu.VMEM((T, seg), BF16),
            pltpu.VMEM((T, seg), BF16),
            pltpu.VMEM((T // HGRN_CHUNK, seg), F32),
            pltpu.VMEM((T, seg), F32),
            pltpu.VMEM((T, seg), F32),
            pltpu.VMEM((T, seg), F32),
            pltpu.VMEM((T, D), F32),
            pltpu.VMEM((T, D), F32),
            pltpu.VMEM((T, seg), BF16),
            pltpu.VMEM((T, seg), BF16),
            pltpu.VMEM((T, D), BF16),
        ],
        compiler_params=pltpu.CompilerParams(
            dimension_semantics=("arbitrary", "arbitrary"),
            vmem_limit_bytes=V7X_VMEM_LIMIT_BYTES),
        name="mixer",
    )(x, *params)


def _ffn_call(x, nfg, w_gu, w_d, nog):
    N, D = x.shape
    T = FFN_TOKEN_TILE
    n_tiles = N // T
    n_carried = FFN_CARRIED_CHUNKS * FFN_COL_CHUNK
    x_spec = pl.BlockSpec((T, D), lambda i: (i, 0))
    x_next_spec = pl.BlockSpec((T, D), lambda i: (jnp.minimum(i + 1, n_tiles - 1), 0))
    params = (nfg, w_gu, w_d, nog)
    return pl.pallas_call(
        _ffn_kernel,
        grid=(n_tiles,),
        in_specs=[x_spec, x_next_spec] + [_resident(p.shape) for p in params],
        out_specs=x_spec,
        out_shape=jax.ShapeDtypeStruct(x.shape, x.dtype),
        scratch_shapes=[
            pltpu.VMEM((T, D), BF16),
            pltpu.VMEM((T, w_d.shape[0] - n_carried), BF16),
            pltpu.VMEM((2, T, n_carried), BF16),
        ],
        compiler_params=pltpu.CompilerParams(
            dimension_semantics=("arbitrary",),
            vmem_limit_bytes=V7X_VMEM_LIMIT_BYTES),
        name="ffn",
    )(x, x, *params)


@jax.jit
def kernel(x, norm_mix_g, w_in, gmlp_ln_g, gmlp_ln_b, gmlp_w_s, gmlp_b_s, hgrn_lb_table,
           hgrn_norm_g, w_branch_a, w_branch_b, w_out, norm_ffn_g, w_gate_up, w_down,
           norm_final_g):
    B, S, D = x.shape
    depth = w_in.shape[0]
    assert depth == 1 and hgrn_lb_table.shape[0] == depth + 1
    assert S % MIXER_SEQ_TILE == 0 and (B * S) % FFN_TOKEN_TILE == 0
    assert w_down.shape[1] % FFN_COL_CHUNK == 0 and D % MIXER_COL_SLAB == 0
    assert D % FFN_CARRIED_CHUNKS == 0
    l = 0
    w_in_p, w_a_p, w_b_p, w_o_p = _pack_call(w_in[l], w_branch_a[l], w_branch_b[l], w_out[l])
    x1 = _mixer_call(
        x, norm_mix_g[l][None, :], w_in_p,
        gmlp_ln_g[l][None, :], gmlp_ln_b[l][None, :], gmlp_w_s[l], gmlp_b_s[l].T,
        hgrn_lb_table, hgrn_norm_g[l][None, :], w_a_p, w_b_p, w_o_p)
    out = _ffn_call(
        x1.reshape(B * S, D), norm_ffn_g[l][None, :], w_gate_up[l].astype(BF16),
        w_down[l].astype(BF16), norm_final_g[None, :])
    return out.reshape(B, S, D)
```

```python
import math

import jax
import jax.numpy as jnp
from jax import lax
from jax.experimental import pallas as pl
from jax.experimental.pallas import tpu as pltpu

F32 = jnp.float32
BF16 = jnp.bfloat16

NORM_EPS = 1e-6
GMLP_GROUPS = 8
GMLP_CHUNK = 128
HGRN_HEADS = 8
HGRN_DK = 128
HGRN_DV = 128
HGRN_CHUNK = 64
HGRN_SCALE = HGRN_DK ** -0.5
N_IN_SEGMENTS = 8

ROW_NORM_MIX, ROW_LN_GAIN, ROW_LN_BIAS, ROW_HGRN_NORM, ROW_LB_TABLE = 0, 1, 2, 3, 4
N_LB_ROWS = 2
ROW_NORM_FFN, ROW_NORM_OUT = 0, 1
VEC_ROWS = 8

MIXER_SEQ_TILE = 256
FFN_TOKEN_TILE = 512
FFN_COL_CHUNK = 256
FFN_CARRIED_CHUNKS = 2
PACK_ROW_BLOCK = 256
MIXER_COL_SLAB = 512
V7X_VMEM_LIMIT_BYTES = 56 * 1024 * 1024


def _silu(x):
    h = 0.5 * x
    return h + h * jnp.tanh(h)


def _gelu_tanh(x):
    c = math.sqrt(2.0 / math.pi)
    h = 0.5 * x
    return h + h * jnp.tanh(x * (c + (c * 0.044715) * (x * x)))


def _rms_norm(x, gain):
    return x * lax.rsqrt(jnp.mean(x * x, axis=-1, keepdims=True) + NORM_EPS) * gain


def _dot(a, b):
    return jnp.dot(a, b, preferred_element_type=F32)


def _dot_nt(a, b):
    return lax.dot_general(a, b, (((1,), (1,)), ((), ())), preferred_element_type=F32)


def _dot_tn(a, b):
    return lax.dot_general(a, b, (((0,), (0,)), ((), ())), preferred_element_type=F32)


def _unpack_rows(packed):
    return pltpu.bitcast(packed, BF16)


def _pack_kernel(*refs):
    n = len(refs) // 2
    for w_ref, o_ref in zip(refs[:n], refs[n:]):
        o_ref[...] = pltpu.bitcast(w_ref[...].astype(BF16), jnp.uint32)


def _mixer_kernel(x_ref, vec_ref, win_ref, ws_ref, bst_ref, wa_ref, wb_ref, wo_ref, o_ref,
                  state_ref, qin_ref, kin_ref, qdec_ref, kdec_ref, v_ref, decay_ref, gs_ref,
                  gu_ref, gv_ref, tga_ref, tgb_ref, a_ref, og_ref, mg_ref):
    T, D = x_ref.shape
    seg = win_ref.shape[1] // N_IN_SEGMENTS
    W = MIXER_COL_SLAB
    n_slabs = seg // W
    C = HGRN_CHUNK
    n_chunks = T // C

    @pl.when(pl.program_id(1) == 0)
    def _():
        state_ref[...] = jnp.zeros_like(state_ref)

    h = _rms_norm(x_ref[...], vec_ref[ROW_NORM_MIX:ROW_NORM_MIX + 1, :]).astype(BF16)

    def proj(i, j):
        c0 = i * seg + j * W
        return _dot(h, _unpack_rows(win_ref[:, c0:c0 + W]))

    row = lax.broadcasted_iota(jnp.int32, (T, T), 0)
    col = lax.broadcasted_iota(jnp.int32, (T, T), 1)
    causal = jnp.logical_and((row >> 6) == (col >> 6), col <= row)
    tri = jnp.where(causal, 1.0, 0.0).astype(BF16)

    lbt = vec_ref[ROW_LB_TABLE:ROW_LB_TABLE + N_LB_ROWS, :]
    lbe = jnp.exp(lbt - jnp.max(lbt, axis=0, keepdims=True))
    lb = lbe[0:1, :] / jnp.sum(lbe, axis=0, keepdims=True)

    gv_sum = jnp.zeros((T, 1), F32)
    f_logits = [proj(3, j) for j in range(n_slabs)]
    gate_vals = []
    for j in range(n_slabs):
        lb_j = lb[:, j * W:(j + 1) * W]
        c1 = 0.5 * (1.0 - lb_j)
        f = (lb_j + c1) + c1 * jnp.tanh(0.5 * f_logits[j])
        log_f = jnp.log(f)
        lf_hi = log_f.astype(BF16)
        lf_lo = (log_f - lf_hi.astype(F32)).astype(BF16)
        gate_vals.append((1.0 - f, _dot(tri, lf_hi) + _dot(tri, lf_lo), proj(2, j) * HGRN_SCALE))
    for j in range(n_slabs):
        js = slice(j * W, (j + 1) * W)
        kk, A, q = gate_vals[j]
        a_mid_rows, e_mid_rows, e_lm_rows = [], [], []
        for c in range(n_chunks):
            a_mid = A[c * C + C // 2 - 1:c * C + C // 2, :]
            a_last = A[c * C + C - 1:c * C + C, :]
            a_mid_rows.append(jnp.broadcast_to(a_mid, (C, W)))
            e_mid_rows.append(jnp.broadcast_to(jnp.exp(a_mid), (C, W)))
            e_lm_rows.append(jnp.broadcast_to(jnp.exp(a_last - a_mid), (C, W)))
            decay_ref[c:c + 1, js] = jnp.exp(a_last)
        rel = A - jnp.concatenate(a_mid_rows, axis=0)
        q_in = q * jnp.exp(rel)
        k_in = kk * jnp.exp(-rel)
        qdec_ref[:, js] = (q_in * jnp.concatenate(e_mid_rows, axis=0)).astype(BF16)
        kdec_ref[:, js] = (k_in * jnp.concatenate(e_lm_rows, axis=0)).astype(BF16)
        qin_ref[:, js] = q_in.astype(BF16)
        kin_ref[:, js] = k_in.astype(BF16)
        v_ref[:, js] = proj(4, j).astype(BF16)
        gs_ref[:, js] = _silu(proj(5, j))
        gu_ref[:, js] = _gelu_tanh(proj(0, j))
        gv = _gelu_tanh(proj(1, j))
        gv_ref[:, js] = gv
        gv_sum = gv_sum + jnp.sum(gv, axis=-1, keepdims=True)

    gate_slabs = [(tga_ref, 6, j) for j in range(n_slabs)] + [(tgb_ref, 7, j) for j in range(n_slabs)]

    def next_gate_slab():
        ref, i, j = gate_slabs.pop(0)
        ref[:, j * W:(j + 1) * W] = jnp.tanh(0.5 * proj(i, j))

    next_gate_slab()
    mu = gv_sum * (1.0 / seg)
    sq_sum = jnp.zeros((T, 1), F32)
    for j in range(n_slabs):
        cen = gv_ref[:, j * W:(j + 1) * W] - mu
        sq_sum = sq_sum + jnp.sum(cen * cen, axis=-1, keepdims=True)
    rstd = lax.rsqrt(sq_sum * (1.0 / seg) + NORM_EPS)
    c_row = lax.broadcasted_iota(jnp.int32, (GMLP_CHUNK, GMLP_CHUNK), 0)
    c_col = lax.broadcasted_iota(jnp.int32, (GMLP_CHUNK, GMLP_CHUNK), 1)
    gd = seg // GMLP_GROUPS

    head_vals = {}

    def head_scores(hd):
        ks = slice(hd * HGRN_DK, (hd + 1) * HGRN_DK)
        vs = slice(hd * HGRN_DV, (hd + 1) * HGRN_DV)
        raw = _dot_nt(qin_ref[:, ks], kin_ref[:, ks])
        d_state = [_dot_tn(v_ref[c * C:(c + 1) * C, vs], kdec_ref[c * C:(c + 1) * C, ks])
                   for c in range(n_chunks)]
        head_vals[hd] = (raw, d_state)

    def head_outputs(hd):
        ks = slice(hd * HGRN_DK, (hd + 1) * HGRN_DK)
        vs = slice(hd * HGRN_DV, (hd + 1) * HGRN_DV)
        raw, d_state = head_vals[hd]
        scores = jnp.where(causal, raw, 0.0).astype(BF16)
        o_intra = _dot(scores, v_ref[:, vs])
        st = state_ref[hd]
        o_inter = []
        for c in range(n_chunks):
            o_inter.append(_dot_nt(qdec_ref[c * C:(c + 1) * C, ks], st.astype(BF16)))
            st = st * decay_ref[c:c + 1, ks] + d_state[c]
        state_ref[hd] = st
        head_vals[hd] = (o_intra, o_inter)

    def head_norm(hd):
        vs = slice(hd * HGRN_DV, (hd + 1) * HGRN_DV)
        o_intra, o_inter = head_vals.pop(hd)
        o_h = o_intra + jnp.concatenate(o_inter, axis=0)
        o_h = o_h * lax.rsqrt(jnp.mean(o_h * o_h, axis=-1, keepdims=True) + NORM_EPS)
        og_ref[:, vs] = (o_h * vec_ref[ROW_HGRN_NORM:ROW_HGRN_NORM + 1, vs] * gs_ref[:, vs]).astype(BF16)

    def gmlp_group(g):
        cs = slice(g * gd, (g + 1) * gd)
        w_g = jnp.where(c_col <= c_row, ws_ref[g], 0.0).astype(BF16)
        bias = bst_ref[:, g:g + 1]
        lnv = ((gv_ref[:, cs] - mu) * rstd * vec_ref[ROW_LN_GAIN:ROW_LN_GAIN + 1, cs]
               + vec_ref[ROW_LN_BIAS:ROW_LN_BIAS + 1, cs]).astype(BF16)
        for n in range(T // GMLP_CHUNK):
            rs = slice(n * GMLP_CHUNK, (n + 1) * GMLP_CHUNK)
            mixed = _dot(w_g, lnv[rs, :]) + bias
            a_ref[rs, cs] = (gu_ref[rs, cs] * mixed).astype(BF16)

    for it in range(HGRN_HEADS + 2):
        if it < HGRN_HEADS:
            head_scores(it)
        if 0 <= it - 1 < HGRN_HEADS:
            head_outputs(it - 1)
        if 0 <= it - 2 < HGRN_HEADS:
            head_norm(it - 2)
        if it < GMLP_GROUPS:
            gmlp_group(it)
        if it % 2 == 1 and gate_slabs:
            next_gate_slab()
    assert not gate_slabs and not head_vals

    for j in range(D // W):
        js = slice(j * W, (j + 1) * W)
        y_a = _dot(a_ref[...], _unpack_rows(wa_ref[:, js]))
        y_b = _dot(og_ref[...], _unpack_rows(wb_ref[:, js]))
        gated = (y_a + tga_ref[:, js] * y_a) + (y_b + tgb_ref[:, js] * y_b)
        mg_ref[:, js] = (0.5 * gated).astype(BF16)
    for j in range(D // W):
        js = slice(j * W, (j + 1) * W)
        o_ref[:, js] = x_ref[:, js] + _dot(mg_ref[...], _unpack_rows(wo_ref[:, js]))


def _ffn_kernel(x_ref, xn_ref, vec_ref, wgu_ref, wd_ref, o_ref, h_ref, act_ref, carry_ref):
    d_ff = wd_ref.shape[0]
    Wc = FFN_COL_CHUNK
    n_carry = FFN_CARRIED_CHUNKS

    read_slot = pl.program_id(0) % 2
    write_slot = 1 - read_slot

    def act_chunk(j):
        gate = _dot(h_ref[...], wgu_ref[:, j * Wc:(j + 1) * Wc])
        up = _dot(h_ref[...], wgu_ref[:, d_ff + j * Wc:d_ff + (j + 1) * Wc])
        return (_silu(gate) * up).astype(BF16)

    @pl.when(pl.program_id(0) == 0)
    def _():
        h_ref[...] = _rms_norm(x_ref[...], vec_ref[ROW_NORM_FFN:ROW_NORM_FFN + 1, :]).astype(BF16)
        for j in range(n_carry):
            carry_ref[0, :, j * Wc:(j + 1) * Wc] = act_chunk(j)

    for j in range(n_carry, d_ff // Wc):
        act_ref[:, (j - n_carry) * Wc:(j - n_carry + 1) * Wc] = act_chunk(j)
    h_ref[...] = _rms_norm(xn_ref[...], vec_ref[ROW_NORM_FFN:ROW_NORM_FFN + 1, :]).astype(BF16)
    part = wd_ref.shape[1] // n_carry
    y_parts = []
    for j in range(n_carry):
        act = jnp.concatenate([carry_ref[read_slot], act_ref[...]], axis=1)
        y_parts.append(_dot(act, wd_ref[:, j * part:(j + 1) * part]))
        carry_ref[write_slot, :, j * Wc:(j + 1) * Wc] = act_chunk(j)
    x2 = x_ref[...] + jnp.concatenate(y_parts, axis=1)
    o_ref[...] = _rms_norm(x2, vec_ref[ROW_NORM_OUT:ROW_NORM_OUT + 1, :])


def _pack_call(*weights):
    k = weights[0].shape[0]
    rows = PACK_ROW_BLOCK
    assert all(w.shape[0] == k for w in weights) and k % rows == 0
    return pl.pallas_call(
        _pack_kernel,
        grid=(k // rows,),
        in_specs=[pl.BlockSpec((rows, w.shape[1]), lambda i: (i, 0)) for w in weights],
        out_specs=[pl.BlockSpec((rows // 2, w.shape[1]), lambda i: (i, 0)) for w in weights],
        out_shape=[jax.ShapeDtypeStruct((k // 2, w.shape[1]), jnp.uint32) for w in weights],
        compiler_params=pltpu.CompilerParams(
            dimension_semantics=("arbitrary",),
            vmem_limit_bytes=V7X_VMEM_LIMIT_BYTES),
        name="pack_weights",
    )(*weights)


def _stack_rows(width, rows):
    out = jnp.zeros((VEC_ROWS, width), F32)
    for r, v in rows.items():
        v = v.reshape(-1, width).astype(F32)
        out = out.at[r:r + v.shape[0], :].set(v)
    return out


def _resident(shape):
    n = len(shape)
    return pl.BlockSpec(shape, lambda *_: (0,) * n, pipeline_mode=pl.Buffered(1))


def _mixer_call(x, vec, w_in, w_s, bst, w_a, w_b, w_o):
    B, S, D = x.shape
    T = MIXER_SEQ_TILE
    seg = w_in.shape[1] // N_IN_SEGMENTS
    x_spec = pl.BlockSpec((None, T, D), lambda b, s: (b, s, 0))
    params = (vec, w_in, w_s, bst, w_a, w_b, w_o)
    return pl.pallas_call(
        _mixer_kernel,
        grid=(B, S // T),
        in_specs=[x_spec] + [_resident(p.shape) for p in params],
        out_specs=x_spec,
        out_shape=jax.ShapeDtypeStruct(x.shape, x.dtype),
        scratch_shapes=[
            pltpu.VMEM((HGRN_HEADS, HGRN_DV, HGRN_DK), F32),
            pltpu.VMEM((T, seg), BF16),
            pltpu.VMEM((T, seg), BF16),
            pltpu.VMEM((T, seg), BF16),
            pltpu.VMEM((T, seg), BF16),
            pltpu.VMEM((T, seg), BF16),
            pltpu.VMEM((T // HGRN_CHUNK, seg), F32),
            pltpu.VMEM((T, seg), F32),
            pltpu.VMEM((T, seg), F32),
            pltpu.VMEM((T, seg), F32),
            pltpu.VMEM((T, D), F32),
            pltpu.VMEM((T, D), F32),
            pltpu.VMEM((T, seg), BF16),
            pltpu.VMEM((T, seg), BF16),
            pltpu.VMEM((T, D), BF16),
        ],
        compiler_params=pltpu.CompilerParams(
            dimension_semantics=("arbitrary", "arbitrary"),
            vmem_limit_bytes=V7X_VMEM_LIMIT_BYTES),
        name="mixer",
    )(x, *params)


def _ffn_call(x, vec, w_gu, w_d):
    N, D = x.shape
    T = FFN_TOKEN_TILE
    n_tiles = N // T
    n_carried = FFN_CARRIED_CHUNKS * FFN_COL_CHUNK
    x_spec = pl.BlockSpec((T, D), lambda i: (i, 0))
    x_next_spec = pl.BlockSpec((T, D), lambda i: (jnp.minimum(i + 1, n_tiles - 1), 0))
    params = (vec, w_gu, w_d)
    return pl.pallas_call(
        _ffn_kernel,
        grid=(n_tiles,),
        in_specs=[x_spec, x_next_spec] + [_resident(p.shape) for p in params],
        out_specs=x_spec,
        out_shape=jax.ShapeDtypeStruct(x.shape, x.dtype),
        scratch_shapes=[
            pltpu.VMEM((T, D), BF16),
            pltpu.VMEM((T, w_d.shape[0] - n_carried), BF16),
            pltpu.VMEM((2, T, n_carried), BF16),
        ],
        compiler_params=pltpu.CompilerParams(
            dimension_semantics=("arbitrary",),
            vmem_limit_bytes=V7X_VMEM_LIMIT_BYTES),
        name="ffn",
    )(x, x, *params)


@jax.jit
def kernel(x, norm_mix_g, w_in, gmlp_ln_g, gmlp_ln_b, gmlp_w_s, gmlp_b_s, hgrn_lb_table,
           hgrn_norm_g, w_branch_a, w_branch_b, w_out, norm_ffn_g, w_gate_up, w_down,
           norm_final_g):
    B, S, D = x.shape
    depth = w_in.shape[0]
    assert depth == 1 and hgrn_lb_table.shape[0] == N_LB_ROWS == depth + 1
    assert gmlp_ln_g.shape[1] == hgrn_norm_g.shape[1] == hgrn_lb_table.shape[1] == D
    assert S % MIXER_SEQ_TILE == 0 and (B * S) % FFN_TOKEN_TILE == 0
    assert w_down.shape[1] % FFN_COL_CHUNK == 0 and D % MIXER_COL_SLAB == 0
    assert D % FFN_CARRIED_CHUNKS == 0
    l = 0
    w_in_p, w_a_p, w_b_p, w_o_p = _pack_call(w_in[l], w_branch_a[l], w_branch_b[l], w_out[l])
    mixer_vec = _stack_rows(D, {ROW_NORM_MIX: norm_mix_g[l], ROW_LN_GAIN: gmlp_ln_g[l],
                               ROW_LN_BIAS: gmlp_ln_b[l], ROW_HGRN_NORM: hgrn_norm_g[l],
                               ROW_LB_TABLE: hgrn_lb_table})
    x1 = _mixer_call(x, mixer_vec, w_in_p, gmlp_w_s[l], gmlp_b_s[l].T, w_a_p, w_b_p, w_o_p)
    ffn_vec = _stack_rows(D, {ROW_NORM_FFN: norm_ffn_g[l], ROW_NORM_OUT: norm_final_g})
    out = _ffn_call(x1.reshape(B * S, D), ffn_vec, w_gate_up[l].astype(BF16), w_down[l].astype(BF16))
    return out.reshape(B, S, D)
```

```python
import math

import jax
import jax.numpy as jnp
from jax import lax
from jax.experimental import pallas as pl
from jax.experimental.pallas import tpu as pltpu

F32 = jnp.float32
BF16 = jnp.bfloat16

NORM_EPS = 1e-6
GMLP_GROUPS = 8
GMLP_CHUNK = 128
HGRN_HEADS = 8
HGRN_DK = 128
HGRN_DV = 128
HGRN_CHUNK = 64
HGRN_CHUNK_SHIFT = HGRN_CHUNK.bit_length() - 1
assert 1 << HGRN_CHUNK_SHIFT == HGRN_CHUNK
GELU_CUBIC_COEFF = 0.044715
HGRN_SCALE = HGRN_DK ** -0.5
N_IN_SEGMENTS = 8

MIXER_SEQ_TILE = 256
FFN_TOKEN_TILE = 512
FFN_COL_CHUNK = 256
FFN_CARRIED_CHUNKS = 2
PACK_ROW_BLOCK = 256
MIXER_COL_SLAB = 512
V7X_VMEM_LIMIT_BYTES = 56 * 1024 * 1024


def _silu(x):
    h = 0.5 * x
    return h + h * jnp.tanh(h)


def _gelu_tanh(x):
    c = math.sqrt(2.0 / math.pi)
    h = 0.5 * x
    return h + h * jnp.tanh(x * (c + (c * GELU_CUBIC_COEFF) * (x * x)))


def _rms_norm(x, gain):
    return x * lax.rsqrt(jnp.mean(x * x, axis=-1, keepdims=True) + NORM_EPS) * gain


def _dot(a, b):
    return jnp.dot(a, b, preferred_element_type=F32)


def _dot_nt(a, b):
    return lax.dot_general(a, b, (((1,), (1,)), ((), ())), preferred_element_type=F32)


def _dot_tn(a, b):
    return lax.dot_general(a, b, (((0,), (0,)), ((), ())), preferred_element_type=F32)


def _unpack_rows(packed):
    return pltpu.bitcast(packed, BF16)


def _pack_kernel(*refs):
    n = len(refs) // 2
    for w_ref, o_ref in zip(refs[:n], refs[n:]):
        o_ref[...] = pltpu.bitcast(w_ref[...].astype(BF16), jnp.uint32)


def _mixer_kernel(x_ref, nmg_ref, win_ref, lng_ref, lnb_ref, ws_ref, bst_ref, lbt_ref,
                  hng_ref, wa_ref, wb_ref, wo_ref, o_ref,
                  state_ref, qin_ref, kin_ref, qdec_ref, kdec_ref, v_ref, decay_ref, gs_ref,
                  gu_ref, gv_ref, tga_ref, tgb_ref, a_ref, og_ref, mg_ref):
    T, D = x_ref.shape
    seg = win_ref.shape[1] // N_IN_SEGMENTS
    W = MIXER_COL_SLAB
    n_slabs = seg // W
    C = HGRN_CHUNK
    n_chunks = T // C

    @pl.when(pl.program_id(1) == 0)
    def _():
        state_ref[...] = jnp.zeros_like(state_ref)

    h = _rms_norm(x_ref[...], nmg_ref[...]).astype(BF16)

    def proj(i, j):
        c0 = i * seg + j * W
        return _dot(h, _unpack_rows(win_ref[:, c0:c0 + W]))

    row = lax.broadcasted_iota(jnp.int32, (T, T), 0)
    col = lax.broadcasted_iota(jnp.int32, (T, T), 1)
    same_chunk = (row >> HGRN_CHUNK_SHIFT) == (col >> HGRN_CHUNK_SHIFT)
    causal = jnp.logical_and(same_chunk, col <= row)
    tri = jnp.where(causal, 1.0, 0.0).astype(BF16)

    lbt = lbt_ref[...]
    lbe = jnp.exp(lbt - jnp.max(lbt, axis=0, keepdims=True))
    lb = lbe[0:1, :] / jnp.sum(lbe, axis=0, keepdims=True)

    gv_sum = jnp.zeros((T, 1), F32)
    f_logits = [proj(3, j) for j in range(n_slabs)]
    gate_vals = []
    for j in range(n_slabs):
        lb_j = lb[:, j * W:(j + 1) * W]
        c1 = 0.5 * (1.0 - lb_j)
        f = (lb_j + c1) + c1 * jnp.tanh(0.5 * f_logits[j])
        log_f = jnp.log(f)
        lf_hi = log_f.astype(BF16)
        lf_lo = (log_f - lf_hi.astype(F32)).astype(BF16)
        gate_vals.append((1.0 - f, _dot(tri, lf_hi) + _dot(tri, lf_lo), proj(2, j) * HGRN_SCALE))
    for j in range(n_slabs):
        js = slice(j * W, (j + 1) * W)
        kk, A, q = gate_vals[j]
        a_mid_rows, e_mid_rows, e_lm_rows = [], [], []
        for c in range(n_chunks):
            a_mid = A[c * C + C // 2 - 1:c * C + C // 2, :]
            a_last = A[c * C + C - 1:c * C + C, :]
            a_mid_rows.append(jnp.broadcast_to(a_mid, (C, W)))
            e_mid_rows.append(jnp.broadcast_to(jnp.exp(a_mid), (C, W)))
            e_lm_rows.append(jnp.broadcast_to(jnp.exp(a_last - a_mid), (C, W)))
            decay_ref[c:c + 1, js] = jnp.exp(a_last)
        rel = A - jnp.concatenate(a_mid_rows, axis=0)
        q_in = q * jnp.exp(rel)
        k_in = kk * jnp.exp(-rel)
        qdec_ref[:, js] = (q_in * jnp.concatenate(e_mid_rows, axis=0)).astype(BF16)
        kdec_ref[:, js] = (k_in * jnp.concatenate(e_lm_rows, axis=0)).astype(BF16)
        qin_ref[:, js] = q_in.astype(BF16)
        kin_ref[:, js] = k_in.astype(BF16)
        v_ref[:, js] = proj(4, j).astype(BF16)
        gs_ref[:, js] = _silu(proj(5, j))
        gu_ref[:, js] = _gelu_tanh(proj(0, j))
        gv = _gelu_tanh(proj(1, j))
        gv_ref[:, js] = gv
        gv_sum = gv_sum + jnp.sum(gv, axis=-1, keepdims=True)

    gate_slabs = [(tga_ref, 6, j) for j in range(n_slabs)] + [(tgb_ref, 7, j) for j in range(n_slabs)]

    def next_gate_slab():
        ref, i, j = gate_slabs.pop(0)
        ref[:, j * W:(j + 1) * W] = jnp.tanh(0.5 * proj(i, j))

    next_gate_slab()
    mu = gv_sum * (1.0 / seg)
    sq_sum = jnp.zeros((T, 1), F32)
    for j in range(n_slabs):
        cen = gv_ref[:, j * W:(j + 1) * W] - mu
        sq_sum = sq_sum + jnp.sum(cen * cen, axis=-1, keepdims=True)
    rstd = lax.rsqrt(sq_sum * (1.0 / seg) + NORM_EPS)
    c_row = lax.broadcasted_iota(jnp.int32, (GMLP_CHUNK, GMLP_CHUNK), 0)
    c_col = lax.broadcasted_iota(jnp.int32, (GMLP_CHUNK, GMLP_CHUNK), 1)
    gd = seg // GMLP_GROUPS

    head_vals = {}

    def head_scores(hd):
        ks = slice(hd * HGRN_DK, (hd + 1) * HGRN_DK)
        vs = slice(hd * HGRN_DV, (hd + 1) * HGRN_DV)
        raw = _dot_nt(qin_ref[:, ks], kin_ref[:, ks])
        d_state = [_dot_tn(v_ref[c * C:(c + 1) * C, vs], kdec_ref[c * C:(c + 1) * C, ks])
                   for c in range(n_chunks)]
        head_vals[hd] = (raw, d_state)

    def head_outputs(hd):
        ks = slice(hd * HGRN_DK, (hd + 1) * HGRN_DK)
        vs = slice(hd * HGRN_DV, (hd + 1) * HGRN_DV)
        raw, d_state = head_vals[hd]
        scores = jnp.where(causal, raw, 0.0).astype(BF16)
        o_intra = _dot(scores, v_ref[:, vs])
        st = state_ref[hd]
        o_inter = []
        for c in range(n_chunks):
            o_inter.append(_dot_nt(qdec_ref[c * C:(c + 1) * C, ks], st.astype(BF16)))
            st = st * decay_ref[c:c + 1, ks] + d_state[c]
        state_ref[hd] = st
        head_vals[hd] = (o_intra, o_inter)

    def head_norm(hd):
        vs = slice(hd * HGRN_DV, (hd + 1) * HGRN_DV)
        o_intra, o_inter = head_vals.pop(hd)
        o_h = o_intra + jnp.concatenate(o_inter, axis=0)
        o_h = o_h * lax.rsqrt(jnp.mean(o_h * o_h, axis=-1, keepdims=True) + NORM_EPS)
        og_ref[:, vs] = (o_h * hng_ref[:, vs] * gs_ref[:, vs]).astype(BF16)

    def gmlp_group(g):
        cs = slice(g * gd, (g + 1) * gd)
        w_g = jnp.where(c_col <= c_row, ws_ref[g], 0.0).astype(BF16)
        bias = bst_ref[:, g:g + 1]
        lnv = ((gv_ref[:, cs] - mu) * rstd * lng_ref[:, cs] + lnb_ref[:, cs]).astype(BF16)
        for n in range(T // GMLP_CHUNK):
            rs = slice(n * GMLP_CHUNK, (n + 1) * GMLP_CHUNK)
            mixed = _dot(w_g, lnv[rs, :]) + bias
            a_ref[rs, cs] = (gu_ref[rs, cs] * mixed).astype(BF16)

    for it in range(HGRN_HEADS + 2):
        if it < HGRN_HEADS:
            head_scores(it)
        if 0 <= it - 1 < HGRN_HEADS:
            head_outputs(it - 1)
        if 0 <= it - 2 < HGRN_HEADS:
            head_norm(it - 2)
        if it < GMLP_GROUPS:
            gmlp_group(it)
        if it % 2 == 1 and gate_slabs:
            next_gate_slab()
    assert not gate_slabs and not head_vals

    for j in range(D // W):
        js = slice(j * W, (j + 1) * W)
        y_a = _dot(a_ref[...], _unpack_rows(wa_ref[:, js]))
        y_b = _dot(og_ref[...], _unpack_rows(wb_ref[:, js]))
        gated = (y_a + tga_ref[:, js] * y_a) + (y_b + tgb_ref[:, js] * y_b)
        mg_ref[:, js] = (0.5 * gated).astype(BF16)
    for j in range(D // W):
        js = slice(j * W, (j + 1) * W)
        o_ref[:, js] = x_ref[:, js] + _dot(mg_ref[...], _unpack_rows(wo_ref[:, js]))


def _ffn_kernel(x_ref, xn_ref, nfg_ref, wgu_ref, wd_ref, nog_ref, o_ref, h_ref, act_ref, carry_ref):
    d_ff = wd_ref.shape[0]
    Wc = FFN_COL_CHUNK
    n_carry = FFN_CARRIED_CHUNKS

    read_slot = pl.program_id(0) % 2
    write_slot = 1 - read_slot

    def act_chunk(j):
        gate = _dot(h_ref[...], wgu_ref[:, j * Wc:(j + 1) * Wc])
        up = _dot(h_ref[...], wgu_ref[:, d_ff + j * Wc:d_ff + (j + 1) * Wc])
        return (_silu(gate) * up).astype(BF16)

    @pl.when(pl.program_id(0) == 0)
    def _():
        h_ref[...] = _rms_norm(x_ref[...], nfg_ref[...]).astype(BF16)
        for j in range(n_carry):
            carry_ref[0, :, j * Wc:(j + 1) * Wc] = act_chunk(j)

    for j in range(n_carry, d_ff // Wc):
        act_ref[:, (j - n_carry) * Wc:(j - n_carry + 1) * Wc] = act_chunk(j)
    h_ref[...] = _rms_norm(xn_ref[...], nfg_ref[...]).astype(BF16)
    part = wd_ref.shape[1] // n_carry
    y_parts = []
    for j in range(n_carry):
        act = jnp.concatenate([carry_ref[read_slot], act_ref[...]], axis=1)
        y_parts.append(_dot(act, wd_ref[:, j * part:(j + 1) * part]))
        carry_ref[write_slot, :, j * Wc:(j + 1) * Wc] = act_chunk(j)
    x2 = x_ref[...] + jnp.concatenate(y_parts, axis=1)
    o_ref[...] = _rms_norm(x2, nog_ref[...])


def _pack_call(*weights):
    k = weights[0].shape[0]
    rows = PACK_ROW_BLOCK
    assert all(w.shape[0] == k for w in weights) and k % rows == 0
    return pl.pallas_call(
        _pack_kernel,
        grid=(k // rows,),
        in_specs=[pl.BlockSpec((rows, w.shape[1]), lambda i: (i, 0)) for w in weights],
        out_specs=[pl.BlockSpec((rows // 2, w.shape[1]), lambda i: (i, 0)) for w in weights],
        out_shape=[jax.ShapeDtypeStruct((k // 2, w.shape[1]), jnp.uint32) for w in weights],
        compiler_params=pltpu.CompilerParams(
            dimension_semantics=("arbitrary",),
            vmem_limit_bytes=V7X_VMEM_LIMIT_BYTES),
        name="pack_weights",
    )(*weights)


def _resident(shape):
    n = len(shape)
    return pl.BlockSpec(shape, lambda *_: (0,) * n, pipeline_mode=pl.Buffered(1))


def _mixer_call(x, nmg, w_in, lng, lnb, w_s, bst, lbt, hng, w_a, w_b, w_o):
    B, S, D = x.shape
    T = MIXER_SEQ_TILE
    seg = w_in.shape[1] // N_IN_SEGMENTS
    x_spec = pl.BlockSpec((None, T, D), lambda b, s: (b, s, 0))
    params = (nmg, w_in, lng, lnb, w_s, bst, lbt, hng, w_a, w_b, w_o)
    return pl.pallas_call(
        _mixer_kernel,
        grid=(B, S // T),
        in_specs=[x_spec] + [_resident(p.shape) for p in params],
        out_specs=x_spec,
        out_shape=jax.ShapeDtypeStruct(x.shape, x.dtype),
        scratch_shapes=[
            pltpu.VMEM((HGRN_HEADS, HGRN_DV, HGRN_DK), F32),
            pltpu.VMEM((T, seg), BF16),
            pltpu.VMEM((T, seg), BF16),
            pltpu.VMEM((T, seg), BF16),
            pltpu.VMEM((T, seg), BF16),
            pltpu.VMEM((T, seg), BF16),
            pltpu.VMEM((T // HGRN_CHUNK, seg), F32),
            pltpu.VMEM((T, seg), F32),
            pltpu.VMEM((T, seg), F32),
            pltpu.VMEM((T, seg), F32),
            pltpu.VMEM((T, D), F32),
            pltpu.VMEM((T, D), F32),
            pltpu.VMEM((T, seg), BF16),
            pltpu.VMEM((T, seg), BF16),
            pltpu.VMEM((T, D), BF16),
        ],
        compiler_params=pltpu.CompilerParams(
            dimension_semantics=("arbitrary", "arbitrary"),
            vmem_limit_bytes=V7X_VMEM_LIMIT_BYTES),
        name="mixer",
    )(x, *params)


def _ffn_call(x, nfg, w_gu, w_d, nog):
    N, D = x.shape
    T = FFN_TOKEN_TILE
    n_tiles = N // T
    n_carried = FFN_CARRIED_CHUNKS * FFN_COL_CHUNK
    x_spec = pl.BlockSpec((T, D), lambda i: (i, 0))
    x_next_spec = pl.BlockSpec((T, D), lambda i: (jnp.minimum(i + 1, n_tiles - 1), 0))
    params = (nfg, w_gu, w_d, nog)
    return pl.pallas_call(
        _ffn_kernel,
        grid=(n_tiles,),
        in_specs=[x_spec, x_next_spec] + [_resident(p.shape) for p in params],
        out_specs=x_spec,
        out_shape=jax.ShapeDtypeStruct(x.shape, x.dtype),
        scratch_shapes=[
            pltpu.VMEM((T, D), BF16),
            pltpu.VMEM((T, w_d.shape[0] - n_carried), BF16),
            pltpu.VMEM((2, T, n_carried), BF16),
        ],
        compiler_params=pltpu.CompilerParams(
            dimension_semantics=("arbitrary",),
            vmem_limit_bytes=V7X_VMEM_LIMIT_BYTES),
        name="ffn",
    )(x, x, *params)


@jax.jit
def kernel(x, norm_mix_g, w_in, gmlp_ln_g, gmlp_ln_b, gmlp_w_s, gmlp_b_s, hgrn_lb_table,
           hgrn_norm_g, w_branch_a, w_branch_b, w_out, norm_ffn_g, w_gate_up, w_down,
           norm_final_g):
    B, S, D = x.shape
    depth = w_in.shape[0]
    assert depth == 1 and hgrn_lb_table.shape[0] == depth + 1
    assert S % MIXER_SEQ_TILE == 0 and (B * S) % FFN_TOKEN_TILE == 0
    assert w_down.shape[1] % FFN_COL_CHUNK == 0 and D % MIXER_COL_SLAB == 0
    assert D % FFN_CARRIED_CHUNKS == 0
    l = 0
    w_in_p, w_a_p, w_b_p, w_o_p = _pack_call(w_in[l], w_branch_a[l], w_branch_b[l], w_out[l])
    x1 = _mixer_call(
        x, norm_mix_g[l][None, :], w_in_p,
        gmlp_ln_g[l][None, :], gmlp_ln_b[l][None, :], gmlp_w_s[l], gmlp_b_s[l].T,
        hgrn_lb_table, hgrn_norm_g[l][None, :], w_a_p, w_b_p, w_o_p)
    out = _ffn_call(
        x1.reshape(B * S, D), norm_ffn_g[l][None, :], w_gate_up[l].astype(BF16),
        w_down[l].astype(BF16), norm_final_g[None, :])
    return out.reshape(B, S, D)
```

```python
import math

import jax
import jax.numpy as jnp
from jax import lax
from jax.experimental import pallas as pl
from jax.experimental.pallas import tpu as pltpu

F32 = jnp.float32
BF16 = jnp.bfloat16

NORM_EPS = 1e-6
GMLP_GROUPS = 8
GMLP_CHUNK = 128
HGRN_HEADS = 8
HGRN_DK = 128
HGRN_DV = 128
HGRN_CHUNK = 64
HGRN_CHUNK_SHIFT = HGRN_CHUNK.bit_length() - 1
assert 1 << HGRN_CHUNK_SHIFT == HGRN_CHUNK
GELU_CUBIC_COEFF = 0.044715
HGRN_SCALE = HGRN_DK ** -0.5
N_IN_SEGMENTS = 8

MIXER_SEQ_TILE = 256
MIXER_TILES_PER_STEP = 4
FFN_TOKEN_TILE = 512
FFN_COL_CHUNK = 256
FFN_CARRIED_CHUNKS = 2
PACK_ROW_BLOCK = 256
MIXER_COL_SLAB = 512
V7X_VMEM_LIMIT_BYTES = 56 * 1024 * 1024


def _silu(x):
    h = 0.5 * x
    return h + h * jnp.tanh(h)


def _gelu_tanh(x):
    c = math.sqrt(2.0 / math.pi)
    h = 0.5 * x
    return h + h * jnp.tanh(x * (c + (c * GELU_CUBIC_COEFF) * (x * x)))


def _rms_norm(x, gain):
    return x * lax.rsqrt(jnp.mean(x * x, axis=-1, keepdims=True) + NORM_EPS) * gain


def _dot(a, b):
    return jnp.dot(a, b, preferred_element_type=F32)


def _dot_nt(a, b):
    return lax.dot_general(a, b, (((1,), (1,)), ((), ())), preferred_element_type=F32)


def _dot_tn(a, b):
    return lax.dot_general(a, b, (((0,), (0,)), ((), ())), preferred_element_type=F32)


def _unpack_rows(packed):
    return pltpu.bitcast(packed, BF16)


def _pack_kernel(*refs):
    n = len(refs) // 2
    for w_ref, o_ref in zip(refs[:n], refs[n:]):
        o_ref[...] = pltpu.bitcast(w_ref[...].astype(BF16), jnp.uint32)


def _mixer_kernel(x_ref, *refs):
    params, o_ref, scratch = refs[:11], refs[11], refs[12:]
    T = MIXER_SEQ_TILE
    for sub in range(x_ref.shape[0] // T):
        rows = pl.ds(sub * T, T)
        _mixer_tile(x_ref.at[rows], *params, o_ref.at[rows], *scratch, first_in_step=(sub == 0))


def _mixer_tile(x_ref, nmg_ref, win_ref, lng_ref, lnb_ref, ws_ref, bst_ref, lbt_ref,
                hng_ref, wa_ref, wb_ref, wo_ref, o_ref,
                state_ref, qin_ref, kin_ref, qdec_ref, kdec_ref, v_ref, decay_ref, gs_ref,
                gu_ref, gv_ref, tga_ref, tgb_ref, a_ref, og_ref, mg_ref, *, first_in_step):
    T, D = x_ref.shape
    seg = win_ref.shape[1] // N_IN_SEGMENTS
    W = MIXER_COL_SLAB
    n_slabs = seg // W
    C = HGRN_CHUNK
    n_chunks = T // C

    if first_in_step:
        @pl.when(pl.program_id(1) == 0)
        def _():
            state_ref[...] = jnp.zeros_like(state_ref)

    h = _rms_norm(x_ref[...], nmg_ref[...]).astype(BF16)

    def proj(i, j):
        c0 = i * seg + j * W
        return _dot(h, _unpack_rows(win_ref[:, c0:c0 + W]))

    row = lax.broadcasted_iota(jnp.int32, (T, T), 0)
    col = lax.broadcasted_iota(jnp.int32, (T, T), 1)
    same_chunk = (row >> HGRN_CHUNK_SHIFT) == (col >> HGRN_CHUNK_SHIFT)
    causal = jnp.logical_and(same_chunk, col <= row)
    tri = jnp.where(causal, 1.0, 0.0).astype(BF16)

    lbt = lbt_ref[...]
    lbe = jnp.exp(lbt - jnp.max(lbt, axis=0, keepdims=True))
    lb = lbe[0:1, :] / jnp.sum(lbe, axis=0, keepdims=True)

    gv_sum = jnp.zeros((T, 1), F32)
    f_logits = [proj(3, j) for j in range(n_slabs)]
    gate_vals = []
    for j in range(n_slabs):
        lb_j = lb[:, j * W:(j + 1) * W]
        c1 = 0.5 * (1.0 - lb_j)
        f = (lb_j + c1) + c1 * jnp.tanh(0.5 * f_logits[j])
        log_f = jnp.log(f)
        lf_hi = log_f.astype(BF16)
        lf_lo = (log_f - lf_hi.astype(F32)).astype(BF16)
        gate_vals.append((1.0 - f, _dot(tri, lf_hi) + _dot(tri, lf_lo), proj(2, j) * HGRN_SCALE))
    for j in range(n_slabs):
        js = slice(j * W, (j + 1) * W)
        kk, A, q = gate_vals[j]
        a_mid_rows, e_mid_rows, e_lm_rows = [], [], []
        for c in range(n_chunks):
            a_mid = A[c * C + C // 2 - 1:c * C + C // 2, :]
            a_last = A[c * C + C - 1:c * C + C, :]
            a_mid_rows.append(jnp.broadcast_to(a_mid, (C, W)))
            e_mid_rows.append(jnp.broadcast_to(jnp.exp(a_mid), (C, W)))
            e_lm_rows.append(jnp.broadcast_to(jnp.exp(a_last - a_mid), (C, W)))
            decay_ref[c:c + 1, js] = jnp.exp(a_last)
        rel = A - jnp.concatenate(a_mid_rows, axis=0)
        q_in = q * jnp.exp(rel)
        k_in = kk * jnp.exp(-rel)
        qdec_ref[:, js] = (q_in * jnp.concatenate(e_mid_rows, axis=0)).astype(BF16)
        kdec_ref[:, js] = (k_in * jnp.concatenate(e_lm_rows, axis=0)).astype(BF16)
        qin_ref[:, js] = q_in.astype(BF16)
        kin_ref[:, js] = k_in.astype(BF16)
        v_ref[:, js] = proj(4, j).astype(BF16)
        gs_ref[:, js] = _silu(proj(5, j))
        gu_ref[:, js] = _gelu_tanh(proj(0, j))
        gv = _gelu_tanh(proj(1, j))
        gv_ref[:, js] = gv
        gv_sum = gv_sum + jnp.sum(gv, axis=-1, keepdims=True)

    gate_slabs = [(tga_ref, 6, j) for j in range(n_slabs)] + [(tgb_ref, 7, j) for j in range(n_slabs)]

    def next_gate_slab():
        ref, i, j = gate_slabs.pop(0)
        ref[:, j * W:(j + 1) * W] = jnp.tanh(0.5 * proj(i, j))

    next_gate_slab()
    mu = gv_sum * (1.0 / seg)
    sq_sum = jnp.zeros((T, 1), F32)
    for j in range(n_slabs):
        cen = gv_ref[:, j * W:(j + 1) * W] - mu
        sq_sum = sq_sum + jnp.sum(cen * cen, axis=-1, keepdims=True)
    rstd = lax.rsqrt(sq_sum * (1.0 / seg) + NORM_EPS)
    c_row = lax.broadcasted_iota(jnp.int32, (GMLP_CHUNK, GMLP_CHUNK), 0)
    c_col = lax.broadcasted_iota(jnp.int32, (GMLP_CHUNK, GMLP_CHUNK), 1)
    gd = seg // GMLP_GROUPS

    head_vals = {}

    def head_scores(hd):
        ks = slice(hd * HGRN_DK, (hd + 1) * HGRN_DK)
        vs = slice(hd * HGRN_DV, (hd + 1) * HGRN_DV)
        raw = _dot_nt(qin_ref[:, ks], kin_ref[:, ks])
        d_state = [_dot_tn(v_ref[c * C:(c + 1) * C, vs], kdec_ref[c * C:(c + 1) * C, ks])
                   for c in range(n_chunks)]
        head_vals[hd] = (raw, d_state)

    def head_outputs(hd):
        ks = slice(hd * HGRN_DK, (hd + 1) * HGRN_DK)
        vs = slice(hd * HGRN_DV, (hd + 1) * HGRN_DV)
        raw, d_state = head_vals[hd]
        scores = jnp.where(causal, raw, 0.0).astype(BF16)
        o_intra = _dot(scores, v_ref[:, vs])
        st = state_ref[hd]
        o_inter = []
        for c in range(n_chunks):
            o_inter.append(_dot_nt(qdec_ref[c * C:(c + 1) * C, ks], st.astype(BF16)))
            st = st * decay_ref[c:c + 1, ks] + d_state[c]
        state_ref[hd] = st
        head_vals[hd] = (o_intra, o_inter)

    def head_norm(hd):
        vs = slice(hd * HGRN_DV, (hd + 1) * HGRN_DV)
        o_intra, o_inter = head_vals.pop(hd)
        o_h = o_intra + jnp.concatenate(o_inter, axis=0)
        o_h = o_h * lax.rsqrt(jnp.mean(o_h * o_h, axis=-1, keepdims=True) + NORM_EPS)
        og_ref[:, vs] = (o_h * hng_ref[:, vs] * gs_ref[:, vs]).astype(BF16)

    def gmlp_group(g):
        cs = slice(g * gd, (g + 1) * gd)
        w_g = jnp.where(c_col <= c_row, ws_ref[g], 0.0).astype(BF16)
        bias = bst_ref[:, g:g + 1]
        lnv = ((gv_ref[:, cs] - mu) * rstd * lng_ref[:, cs] + lnb_ref[:, cs]).astype(BF16)
        for n in range(T // GMLP_CHUNK):
            rs = slice(n * GMLP_CHUNK, (n + 1) * GMLP_CHUNK)
            mixed = _dot(w_g, lnv[rs, :]) + bias
            a_ref[rs, cs] = (gu_ref[rs, cs] * mixed).astype(BF16)

    for it in range(HGRN_HEADS + 2):
        if it < HGRN_HEADS:
            head_scores(it)
        if 0 <= it - 1 < HGRN_HEADS:
            head_outputs(it - 1)
        if 0 <= it - 2 < HGRN_HEADS:
            head_norm(it - 2)
        if it < GMLP_GROUPS:
            gmlp_group(it)
        if it % 2 == 1 and gate_slabs:
            next_gate_slab()
    assert not gate_slabs and not head_vals

    for j in range(D // W):
        js = slice(j * W, (j + 1) * W)
        y_a = _dot(a_ref[...], _unpack_rows(wa_ref[:, js]))
        y_b = _dot(og_ref[...], _unpack_rows(wb_ref[:, js]))
        gated = (y_a + tga_ref[:, js] * y_a) + (y_b + tgb_ref[:, js] * y_b)
        mg_ref[:, js] = (0.5 * gated).astype(BF16)
    for j in range(D // W):
        js = slice(j * W, (j + 1) * W)
        o_ref[:, js] = x_ref[:, js] + _dot(mg_ref[...], _unpack_rows(wo_ref[:, js]))


def _ffn_kernel(x_ref, xn_ref, nfg_ref, wgu_ref, wd_ref, nog_ref, o_ref, h_ref, act_ref, carry_ref):
    d_ff = wd_ref.shape[0]
    Wc = FFN_COL_CHUNK
    n_carry = FFN_CARRIED_CHUNKS

    read_slot = pl.program_id(0) % 2
    write_slot = 1 - read_slot

    def act_chunk(j):
        gate = _dot(h_ref[...], wgu_ref[:, j * Wc:(j + 1) * Wc])
        up = _dot(h_ref[...], wgu_ref[:, d_ff + j * Wc:d_ff + (j + 1) * Wc])
        return (_silu(gate) * up).astype(BF16)

    @pl.when(pl.program_id(0) == 0)
    def _():
        h_ref[...] = _rms_norm(x_ref[...], nfg_ref[...]).astype(BF16)
        for j in range(n_carry):
            carry_ref[0, :, j * Wc:(j + 1) * Wc] = act_chunk(j)

    for j in range(n_carry, d_ff // Wc):
        act_ref[:, (j - n_carry) * Wc:(j - n_carry + 1) * Wc] = act_chunk(j)
    h_ref[...] = _rms_norm(xn_ref[...], nfg_ref[...]).astype(BF16)
    part = wd_ref.shape[1] // n_carry
    y_parts = []
    for j in range(n_carry):
        act = jnp.concatenate([carry_ref[read_slot], act_ref[...]], axis=1)
        y_parts.append(_dot(act, wd_ref[:, j * part:(j + 1) * part]))
        carry_ref[write_slot, :, j * Wc:(j + 1) * Wc] = act_chunk(j)
    x2 = x_ref[...] + jnp.concatenate(y_parts, axis=1)
    o_ref[...] = _rms_norm(x2, nog_ref[...])


def _pack_call(*weights):
    k = weights[0].shape[0]
    rows = PACK_ROW_BLOCK
    assert all(w.shape[0] == k for w in weights) and k % rows == 0
    return pl.pallas_call(
        _pack_kernel,
        grid=(k // rows,),
        in_specs=[pl.BlockSpec((rows, w.shape[1]), lambda i: (i, 0)) for w in weights],
        out_specs=[pl.BlockSpec((rows // 2, w.shape[1]), lambda i: (i, 0)) for w in weights],
        out_shape=[jax.ShapeDtypeStruct((k // 2, w.shape[1]), jnp.uint32) for w in weights],
        compiler_params=pltpu.CompilerParams(
            dimension_semantics=("arbitrary",),
            vmem_limit_bytes=V7X_VMEM_LIMIT_BYTES),
        name="pack_weights",
    )(*weights)


def _resident(shape):
    n = len(shape)
    return pl.BlockSpec(shape, lambda *_: (0,) * n, pipeline_mode=pl.Buffered(1))


def _mixer_call(x, nmg, w_in, lng, lnb, w_s, bst, lbt, hng, w_a, w_b, w_o):
    B, S, D = x.shape
    T = MIXER_SEQ_TILE
    seg = w_in.shape[1] // N_IN_SEGMENTS
    rows = T * MIXER_TILES_PER_STEP
    x_spec = pl.BlockSpec((None, rows, D), lambda b, s: (b, s, 0))
    params = (nmg, w_in, lng, lnb, w_s, bst, lbt, hng, w_a, w_b, w_o)
    return pl.pallas_call(
        _mixer_kernel,
        grid=(B, S // rows),
        in_specs=[x_spec] + [_resident(p.shape) for p in params],
        out_specs=x_spec,
        out_shape=jax.ShapeDtypeStruct(x.shape, x.dtype),
        scratch_shapes=[
            pltpu.VMEM((HGRN_HEADS, HGRN_DV, HGRN_DK), F32),
            pltpu.VMEM((T, seg), BF16),
            pltpu.VMEM((T, seg), BF16),
            pltpu.VMEM((T, seg), BF16),
            pltpu.VMEM((T, seg), BF16),
            pltpu.VMEM((T, seg), BF16),
            pltpu.VMEM((T // HGRN_CHUNK, seg), F32),
            pltpu.VMEM((T, seg), F32),
            pltpu.VMEM((T, seg), F32),
            pltpu.VMEM((T, seg), F32),
            pltpu.VMEM((T, D), F32),
            pltpu.VMEM((T, D), F32),
            pltpu.VMEM((T, seg), BF16),
            pltpu.VMEM((T, seg), BF16),
            pltpu.VMEM((T, D), BF16),
        ],
        compiler_params=pltpu.CompilerParams(
            dimension_semantics=("arbitrary", "arbitrary"),
            vmem_limit_bytes=V7X_VMEM_LIMIT_BYTES),
        name="mixer",
    )(x, *params)


def _ffn_call(x, nfg, w_gu, w_d, nog):
    N, D = x.shape
    T = FFN_TOKEN_TILE
    n_tiles = N // T
    n_carried = FFN_CARRIED_CHUNKS * FFN_COL_CHUNK
    x_spec = pl.BlockSpec((T, D), lambda i: (i, 0))
    x_next_spec = pl.BlockSpec((T, D), lambda i: (jnp.minimum(i + 1, n_tiles - 1), 0))
    params = (nfg, w_gu, w_d, nog)
    return pl.pallas_call(
        _ffn_kernel,
        grid=(n_tiles,),
        in_specs=[x_spec, x_next_spec] + [_resident(p.shape) for p in params],
        out_specs=x_spec,
        out_shape=jax.ShapeDtypeStruct(x.shape, x.dtype),
        scratch_shapes=[
            pltpu.VMEM((T, D), BF16),
            pltpu.VMEM((T, w_d.shape[0] - n_carried), BF16),
            pltpu.VMEM((2, T, n_carried), BF16),
        ],
        compiler_params=pltpu.CompilerParams(
            dimension_semantics=("arbitrary",),
            vmem_limit_bytes=V7X_VMEM_LIMIT_BYTES),
        name="ffn",
    )(x, x, *params)


@jax.jit
def kernel(x, norm_mix_g, w_in, gmlp_ln_g, gmlp_ln_b, gmlp_w_s, gmlp_b_s, hgrn_lb_table,
           hgrn_norm_g, w_branch_a, w_branch_b, w_out, norm_ffn_g, w_gate_up, w_down,
           norm_final_g):
    B, S, D = x.shape
    depth = w_in.shape[0]
    assert depth == 1 and hgrn_lb_table.shape[0] == depth + 1
    assert S % (MIXER_SEQ_TILE * MIXER_TILES_PER_STEP) == 0 and (B * S) % FFN_TOKEN_TILE == 0
    assert w_down.shape[1] % FFN_COL_CHUNK == 0 and D % MIXER_COL_SLAB == 0
    assert D % FFN_CARRIED_CHUNKS == 0
    l = 0
    w_in_p, w_a_p, w_b_p, w_o_p = _pack_call(w_in[l], w_branch_a[l], w_branch_b[l], w_out[l])
    x1 = _mixer_call(
        x, norm_mix_g[l][None, :], w_in_p,
        gmlp_ln_g[l][None, :], gmlp_ln_b[l][None, :], gmlp_w_s[l], gmlp_b_s[l].T,
        hgrn_lb_table, hgrn_norm_g[l][None, :], w_a_p, w_b_p, w_o_p)
    out = _ffn_call(
        x1.reshape(B * S, D), norm_ffn_g[l][None, :], w_gate_up[l].astype(BF16),
        w_down[l].astype(BF16), norm_final_g[None, :])
    return out.reshape(B, S, D)
```

```python
import math

import jax
import jax.numpy as jnp
from jax import lax
from jax.experimental import pallas as pl
from jax.experimental.pallas import tpu as pltpu

F32 = jnp.float32
BF16 = jnp.bfloat16

NORM_EPS = 1e-6
GMLP_GROUPS = 8
GMLP_CHUNK = 128
HGRN_HEADS = 8
HGRN_DK = 128
HGRN_DV = 128
HGRN_CHUNK = 64
HGRN_CHUNK_SHIFT = HGRN_CHUNK.bit_length() - 1
assert 1 << HGRN_CHUNK_SHIFT == HGRN_CHUNK
GELU_CUBIC_COEFF = 0.044715
HGRN_SCALE = HGRN_DK ** -0.5
N_IN_SEGMENTS = 8

MIXER_SEQ_TILE = 256
MIXER_TILES_PER_STEP = 2
FFN_TOKEN_TILE = 512
FFN_TILES_PER_STEP = 2
FFN_COL_CHUNK = 256
FFN_CARRIED_CHUNKS = 2
PACK_ROW_BLOCK = 256
MIXER_COL_SLAB = 512
V7X_VMEM_LIMIT_BYTES = 56 * 1024 * 1024


def _silu(x):
    h = 0.5 * x
    return h + h * jnp.tanh(h)


def _gelu_tanh(x):
    c = math.sqrt(2.0 / math.pi)
    h = 0.5 * x
    return h + h * jnp.tanh(x * (c + (c * GELU_CUBIC_COEFF) * (x * x)))


def _rms_norm(x, gain):
    return x * lax.rsqrt(jnp.mean(x * x, axis=-1, keepdims=True) + NORM_EPS) * gain


def _dot(a, b):
    return jnp.dot(a, b, preferred_element_type=F32)


def _dot_nt(a, b):
    return lax.dot_general(a, b, (((1,), (1,)), ((), ())), preferred_element_type=F32)


def _dot_tn(a, b):
    return lax.dot_general(a, b, (((0,), (0,)), ((), ())), preferred_element_type=F32)


def _unpack_rows(packed):
    return pltpu.bitcast(packed, BF16)


def _pack_kernel(*refs):
    n = len(refs) // 2
    for w_ref, o_ref in zip(refs[:n], refs[n:]):
        o_ref[...] = pltpu.bitcast(w_ref[...].astype(BF16), jnp.uint32)


def _mixer_kernel(x_ref, *refs):
    params, o_ref, scratch = refs[:11], refs[11], refs[12:]
    T = MIXER_SEQ_TILE
    for sub in range(x_ref.shape[0] // T):
        rows = pl.ds(sub * T, T)
        _mixer_tile(x_ref.at[rows], *params, o_ref.at[rows], *scratch, first_in_step=(sub == 0))


def _mixer_tile(x_ref, nmg_ref, win_ref, lng_ref, lnb_ref, ws_ref, bst_ref, lbt_ref,
                hng_ref, wa_ref, wb_ref, wo_ref, o_ref,
                state_ref, qin_ref, kin_ref, qdec_ref, kdec_ref, v_ref, decay_ref, gs_ref,
                gu_ref, gv_ref, tga_ref, tgb_ref, a_ref, og_ref, mg_ref, *, first_in_step):
    T, D = x_ref.shape
    seg = win_ref.shape[1] // N_IN_SEGMENTS
    W = MIXER_COL_SLAB
    n_slabs = seg // W
    C = HGRN_CHUNK
    n_chunks = T // C

    if first_in_step:
        @pl.when(pl.program_id(1) == 0)
        def _():
            state_ref[...] = jnp.zeros_like(state_ref)

    h = _rms_norm(x_ref[...], nmg_ref[...]).astype(BF16)

    def proj(i, j):
        c0 = i * seg + j * W
        return _dot(h, _unpack_rows(win_ref[:, c0:c0 + W]))

    row = lax.broadcasted_iota(jnp.int32, (T, T), 0)
    col = lax.broadcasted_iota(jnp.int32, (T, T), 1)
    same_chunk = (row >> HGRN_CHUNK_SHIFT) == (col >> HGRN_CHUNK_SHIFT)
    causal = jnp.logical_and(same_chunk, col <= row)
    tri = jnp.where(causal, 1.0, 0.0).astype(BF16)

    lbt = lbt_ref[...]
    lbe = jnp.exp(lbt - jnp.max(lbt, axis=0, keepdims=True))
    lb = lbe[0:1, :] / jnp.sum(lbe, axis=0, keepdims=True)

    gv_sum = jnp.zeros((T, 1), F32)
    f_logits = [proj(3, j) for j in range(n_slabs)]
    gate_vals = []
    for j in range(n_slabs):
        lb_j = lb[:, j * W:(j + 1) * W]
        c1 = 0.5 * (1.0 - lb_j)
        f = (lb_j + c1) + c1 * jnp.tanh(0.5 * f_logits[j])
        log_f = jnp.log(f)
        lf_hi = log_f.astype(BF16)
        lf_lo = (log_f - lf_hi.astype(F32)).astype(BF16)
        gate_vals.append((1.0 - f, _dot(tri, lf_hi) + _dot(tri, lf_lo), proj(2, j) * HGRN_SCALE))
    for j in range(n_slabs):
        js = slice(j * W, (j + 1) * W)
        kk, A, q = gate_vals[j]
        a_mid_rows, e_mid_rows, e_lm_rows = [], [], []
        for c in range(n_chunks):
            a_mid = A[c * C + C // 2 - 1:c * C + C // 2, :]
            a_last = A[c * C + C - 1:c * C + C, :]
            a_mid_rows.append(jnp.broadcast_to(a_mid, (C, W)))
            e_mid_rows.append(jnp.broadcast_to(jnp.exp(a_mid), (C, W)))
            e_lm_rows.append(jnp.broadcast_to(jnp.exp(a_last - a_mid), (C, W)))
            decay_ref[c:c + 1, js] = jnp.exp(a_last)
        rel = A - jnp.concatenate(a_mid_rows, axis=0)
        q_in = q * jnp.exp(rel)
        k_in = kk * jnp.exp(-rel)
        qdec_ref[:, js] = (q_in * jnp.concatenate(e_mid_rows, axis=0)).astype(BF16)
        kdec_ref[:, js] = (k_in * jnp.concatenate(e_lm_rows, axis=0)).astype(BF16)
        qin_ref[:, js] = q_in.astype(BF16)
        kin_ref[:, js] = k_in.astype(BF16)
        v_ref[:, js] = proj(4, j).astype(BF16)
        gs_ref[:, js] = _silu(proj(5, j))
        gu_ref[:, js] = _gelu_tanh(proj(0, j))
        gv = _gelu_tanh(proj(1, j))
        gv_ref[:, js] = gv
        gv_sum = gv_sum + jnp.sum(gv, axis=-1, keepdims=True)

    gate_slabs = [(tga_ref, 6, j) for j in range(n_slabs)] + [(tgb_ref, 7, j) for j in range(n_slabs)]

    def next_gate_slab():
        ref, i, j = gate_slabs.pop(0)
        ref[:, j * W:(j + 1) * W] = jnp.tanh(0.5 * proj(i, j))

    next_gate_slab()
    mu = gv_sum * (1.0 / seg)
    sq_sum = jnp.zeros((T, 1), F32)
    for j in range(n_slabs):
        cen = gv_ref[:, j * W:(j + 1) * W] - mu
        sq_sum = sq_sum + jnp.sum(cen * cen, axis=-1, keepdims=True)
    rstd = lax.rsqrt(sq_sum * (1.0 / seg) + NORM_EPS)
    c_row = lax.broadcasted_iota(jnp.int32, (GMLP_CHUNK, GMLP_CHUNK), 0)
    c_col = lax.broadcasted_iota(jnp.int32, (GMLP_CHUNK, GMLP_CHUNK), 1)
    gd = seg // GMLP_GROUPS

    head_vals = {}

    def head_scores(hd):
        ks = slice(hd * HGRN_DK, (hd + 1) * HGRN_DK)
        vs = slice(hd * HGRN_DV, (hd + 1) * HGRN_DV)
        raw = _dot_nt(qin_ref[:, ks], kin_ref[:, ks])
        d_state = [_dot_tn(v_ref[c * C:(c + 1) * C, vs], kdec_ref[c * C:(c + 1) * C, ks])
                   for c in range(n_chunks)]
        head_vals[hd] = (raw, d_state)

    def head_outputs(hd):
        ks = slice(hd * HGRN_DK, (hd + 1) * HGRN_DK)
        vs = slice(hd * HGRN_DV, (hd + 1) * HGRN_DV)
        raw, d_state = head_vals[hd]
        scores = jnp.where(causal, raw, 0.0).astype(BF16)
        o_intra = _dot(scores, v_ref[:, vs])
        st = state_ref[hd]
        o_inter = []
        for c in range(n_chunks):
            o_inter.append(_dot_nt(qdec_ref[c * C:(c + 1) * C, ks], st.astype(BF16)))
            st = st * decay_ref[c:c + 1, ks] + d_state[c]
        state_ref[hd] = st
        head_vals[hd] = (o_intra, o_inter)

    def head_norm(hd):
        vs = slice(hd * HGRN_DV, (hd + 1) * HGRN_DV)
        o_intra, o_inter = head_vals.pop(hd)
        o_h = o_intra + jnp.concatenate(o_inter, axis=0)
        o_h = o_h * lax.rsqrt(jnp.mean(o_h * o_h, axis=-1, keepdims=True) + NORM_EPS)
        og_ref[:, vs] = (o_h * hng_ref[:, vs] * gs_ref[:, vs]).astype(BF16)

    def gmlp_group(g):
        cs = slice(g * gd, (g + 1) * gd)
        w_g = jnp.where(c_col <= c_row, ws_ref[g], 0.0).astype(BF16)
        bias = bst_ref[:, g:g + 1]
        lnv = ((gv_ref[:, cs] - mu) * rstd * lng_ref[:, cs] + lnb_ref[:, cs]).astype(BF16)
        for n in range(T // GMLP_CHUNK):
            rs = slice(n * GMLP_CHUNK, (n + 1) * GMLP_CHUNK)
            mixed = _dot(w_g, lnv[rs, :]) + bias
            a_ref[rs, cs] = (gu_ref[rs, cs] * mixed).astype(BF16)

    for it in range(HGRN_HEADS + 2):
        if it < HGRN_HEADS:
            head_scores(it)
        if 0 <= it - 1 < HGRN_HEADS:
            head_outputs(it - 1)
        if 0 <= it - 2 < HGRN_HEADS:
            head_norm(it - 2)
        if it < GMLP_GROUPS:
            gmlp_group(it)
        if it % 2 == 1 and gate_slabs:
            next_gate_slab()
    assert not gate_slabs and not head_vals

    for j in range(D // W):
        js = slice(j * W, (j + 1) * W)
        y_a = _dot(a_ref[...], _unpack_rows(wa_ref[:, js]))
        y_b = _dot(og_ref[...], _unpack_rows(wb_ref[:, js]))
        gated = (y_a + tga_ref[:, js] * y_a) + (y_b + tgb_ref[:, js] * y_b)
        mg_ref[:, js] = (0.5 * gated).astype(BF16)
    for j in range(D // W):
        js = slice(j * W, (j + 1) * W)
        o_ref[:, js] = x_ref[:, js] + _dot(mg_ref[...], _unpack_rows(wo_ref[:, js]))


def _ffn_kernel(x_ref, xn_ref, nfg_ref, wgu_ref, wd_ref, nog_ref, o_ref, h_ref, act_ref, carry_ref):
    T = FFN_TOKEN_TILE
    tiles = [x_ref.at[pl.ds(t * T, T)] for t in range(FFN_TILES_PER_STEP)] + [xn_ref]
    for t in range(FFN_TILES_PER_STEP):
        _ffn_tile(tiles[t], tiles[t + 1], nfg_ref, wgu_ref, wd_ref, nog_ref,
                  o_ref.at[pl.ds(t * T, T)], h_ref, act_ref, carry_ref,
                  read_slot=t % 2, first_in_step=(t == 0))


def _ffn_tile(x_ref, xn_ref, nfg_ref, wgu_ref, wd_ref, nog_ref, o_ref, h_ref, act_ref, carry_ref,
              *, read_slot, first_in_step):
    d_ff = wd_ref.shape[0]
    Wc = FFN_COL_CHUNK
    n_carry = FFN_CARRIED_CHUNKS

    write_slot = 1 - read_slot

    def act_chunk(j):
        gate = _dot(h_ref[...], wgu_ref[:, j * Wc:(j + 1) * Wc])
        up = _dot(h_ref[...], wgu_ref[:, d_ff + j * Wc:d_ff + (j + 1) * Wc])
        return (_silu(gate) * up).astype(BF16)

    if first_in_step:
        @pl.when(pl.program_id(0) == 0)
        def _():
            h_ref[...] = _rms_norm(x_ref[...], nfg_ref[...]).astype(BF16)
            for j in range(n_carry):
                carry_ref[read_slot, :, j * Wc:(j + 1) * Wc] = act_chunk(j)

    for j in range(n_carry, d_ff // Wc):
        act_ref[:, (j - n_carry) * Wc:(j - n_carry + 1) * Wc] = act_chunk(j)
    h_ref[...] = _rms_norm(xn_ref[...], nfg_ref[...]).astype(BF16)
    part = wd_ref.shape[1] // n_carry
    y_parts = []
    for j in range(n_carry):
        act = jnp.concatenate([carry_ref[read_slot], act_ref[...]], axis=1)
        y_parts.append(_dot(act, wd_ref[:, j * part:(j + 1) * part]))
        carry_ref[write_slot, :, j * Wc:(j + 1) * Wc] = act_chunk(j)
    x2 = x_ref[...] + jnp.concatenate(y_parts, axis=1)
    o_ref[...] = _rms_norm(x2, nog_ref[...])


def _pack_call(*weights):
    k = weights[0].shape[0]
    rows = PACK_ROW_BLOCK
    assert all(w.shape[0] == k for w in weights) and k % rows == 0
    return pl.pallas_call(
        _pack_kernel,
        grid=(k // rows,),
        in_specs=[pl.BlockSpec((rows, w.shape[1]), lambda i: (i, 0)) for w in weights],
        out_specs=[pl.BlockSpec((rows // 2, w.shape[1]), lambda i: (i, 0)) for w in weights],
        out_shape=[jax.ShapeDtypeStruct((k // 2, w.shape[1]), jnp.uint32) for w in weights],
        compiler_params=pltpu.CompilerParams(
            dimension_semantics=("arbitrary",),
            vmem_limit_bytes=V7X_VMEM_LIMIT_BYTES),
        name="pack_weights",
    )(*weights)


def _resident(shape):
    n = len(shape)
    return pl.BlockSpec(shape, lambda *_: (0,) * n, pipeline_mode=pl.Buffered(1))


def _mixer_call(x, nmg, w_in, lng, lnb, w_s, bst, lbt, hng, w_a, w_b, w_o):
    B, S, D = x.shape
    T = MIXER_SEQ_TILE
    seg = w_in.shape[1] // N_IN_SEGMENTS
    rows = T * MIXER_TILES_PER_STEP
    x_spec = pl.BlockSpec((None, rows, D), lambda b, s: (b, s, 0))
    params = (nmg, w_in, lng, lnb, w_s, bst, lbt, hng, w_a, w_b, w_o)
    return pl.pallas_call(
        _mixer_kernel,
        grid=(B, S // rows),
        in_specs=[x_spec] + [_resident(p.shape) for p in params],
        out_specs=x_spec,
        out_shape=jax.ShapeDtypeStruct(x.shape, x.dtype),
        scratch_shapes=[
            pltpu.VMEM((HGRN_HEADS, HGRN_DV, HGRN_DK), F32),
            pltpu.VMEM((T, seg), BF16),
            pltpu.VMEM((T, seg), BF16),
            pltpu.VMEM((T, seg), BF16),
            pltpu.VMEM((T, seg), BF16),
            pltpu.VMEM((T, seg), BF16),
            pltpu.VMEM((T // HGRN_CHUNK, seg), F32),
            pltpu.VMEM((T, seg), F32),
            pltpu.VMEM((T, seg), F32),
            pltpu.VMEM((T, seg), F32),
            pltpu.VMEM((T, D), F32),
            pltpu.VMEM((T, D), F32),
            pltpu.VMEM((T, seg), BF16),
            pltpu.VMEM((T, seg), BF16),
            pltpu.VMEM((T, D), BF16),
        ],
        compiler_params=pltpu.CompilerParams(
            dimension_semantics=("arbitrary", "arbitrary"),
            vmem_limit_bytes=V7X_VMEM_LIMIT_BYTES),
        name="mixer",
    )(x, *params)


def _ffn_call(x, nfg, w_gu, w_d, nog):
    N, D = x.shape
    T = FFN_TOKEN_TILE
    n_tiles = N // T
    n_steps = n_tiles // FFN_TILES_PER_STEP
    n_carried = FFN_CARRIED_CHUNKS * FFN_COL_CHUNK
    x_spec = pl.BlockSpec((T * FFN_TILES_PER_STEP, D), lambda i: (i, 0))
    x_next_spec = pl.BlockSpec(
        (T, D), lambda i: (jnp.minimum((i + 1) * FFN_TILES_PER_STEP, n_tiles - 1), 0))
    params = (nfg, w_gu, w_d, nog)
    return pl.pallas_call(
        _ffn_kernel,
        grid=(n_steps,),
        in_specs=[x_spec, x_next_spec] + [_resident(p.shape) for p in params],
        out_specs=x_spec,
        out_shape=jax.ShapeDtypeStruct(x.shape, x.dtype),
        scratch_shapes=[
            pltpu.VMEM((T, D), BF16),
            pltpu.VMEM((T, w_d.shape[0] - n_carried), BF16),
            pltpu.VMEM((2, T, n_carried), BF16),
        ],
        compiler_params=pltpu.CompilerParams(
            dimension_semantics=("arbitrary",),
            vmem_limit_bytes=V7X_VMEM_LIMIT_BYTES),
        name="ffn",
    )(x, x, *params)


@jax.jit
def kernel(x, norm_mix_g, w_in, gmlp_ln_g, gmlp_ln_b, gmlp_w_s, gmlp_b_s, hgrn_lb_table,
           hgrn_norm_g, w_branch_a, w_branch_b, w_out, norm_ffn_g, w_gate_up, w_down,
           norm_final_g):
    B, S, D = x.shape
    depth = w_in.shape[0]
    assert depth == 1 and hgrn_lb_table.shape[0] == depth + 1
    assert S % (MIXER_SEQ_TILE * MIXER_TILES_PER_STEP) == 0 and (B * S) % (FFN_TOKEN_TILE * FFN_TILES_PER_STEP) == 0
    assert FFN_TILES_PER_STEP == 2
    assert w_down.shape[1] % FFN_COL_CHUNK == 0 and D % MIXER_COL_SLAB == 0
    assert D % FFN_CARRIED_CHUNKS == 0
    l = 0
    w_in_p, w_a_p, w_b_p, w_o_p = _pack_call(w_in[l], w_branch_a[l], w_branch_b[l], w_out[l])
    x1 = _mixer_call(
        x, norm_mix_g[l][None, :], w_in_p,
        gmlp_ln_g[l][None, :], gmlp_ln_b[l][None, :], gmlp_w_s[l], gmlp_b_s[l].T,
        hgrn_lb_table, hgrn_norm_g[l][None, :], w_a_p, w_b_p, w_o_p)
    out = _ffn_call(
        x1.reshape(B * S, D), norm_ffn_g[l][None, :], w_gate_up[l].astype(BF16),
        w_down[l].astype(BF16), norm_final_g[None, :])
    return out.reshape(B, S, D)
```

```python
import math

import jax
import jax.numpy as jnp
from jax import lax
from jax.experimental import pallas as pl
from jax.experimental.pallas import tpu as pltpu

F32 = jnp.float32
BF16 = jnp.bfloat16

NORM_EPS = 1e-6
GMLP_GROUPS = 8
GMLP_CHUNK = 128
HGRN_HEADS = 8
HGRN_DK = 128
HGRN_DV = 128
HGRN_CHUNK = 64
HGRN_CHUNK_SHIFT = HGRN_CHUNK.bit_length() - 1
assert 1 << HGRN_CHUNK_SHIFT == HGRN_CHUNK
GELU_CUBIC_COEFF = 0.044715
HGRN_SCALE = HGRN_DK ** -0.5
N_IN_SEGMENTS = 8

MIXER_SEQ_TILE = 256
MIXER_TILES_PER_STEP = 2
FFN_TOKEN_TILE = 512
FFN_TILES_PER_STEP = 2
FFN_COL_CHUNK = 256
FFN_CARRIED_CHUNKS = 2
PACK_ROW_BLOCK = 256
CONVERT_GRID_STEPS = 8
MIXER_COL_SLAB = 512
V7X_VMEM_LIMIT_BYTES = 56 * 1024 * 1024


def _silu(x):
    h = 0.5 * x
    return h + h * jnp.tanh(h)


def _gelu_tanh(x):
    c = math.sqrt(2.0 / math.pi)
    h = 0.5 * x
    return h + h * jnp.tanh(x * (c + (c * GELU_CUBIC_COEFF) * (x * x)))


def _rms_norm(x, gain):
    return x * lax.rsqrt(jnp.mean(x * x, axis=-1, keepdims=True) + NORM_EPS) * gain


def _dot(a, b):
    return jnp.dot(a, b, preferred_element_type=F32)


def _dot_nt(a, b):
    return lax.dot_general(a, b, (((1,), (1,)), ((), ())), preferred_element_type=F32)


def _dot_tn(a, b):
    return lax.dot_general(a, b, (((0,), (0,)), ((), ())), preferred_element_type=F32)


def _unpack_rows(packed):
    return pltpu.bitcast(packed, BF16)


def _pack_kernel(*refs):
    n = len(refs) // 2
    for w_ref, o_ref in zip(refs[:n], refs[n:]):
        w = w_ref[...].astype(BF16)
        o_ref[...] = pltpu.bitcast(w, jnp.uint32) if o_ref.dtype == jnp.uint32 else w


def _mixer_kernel(x_ref, *refs):
    params, o_ref, scratch = refs[:11], refs[11], refs[12:]
    T = MIXER_SEQ_TILE
    for sub in range(x_ref.shape[0] // T):
        rows = pl.ds(sub * T, T)
        _mixer_tile(x_ref.at[rows], *params, o_ref.at[rows], *scratch, first_in_step=(sub == 0))


def _mixer_tile(x_ref, nmg_ref, win_ref, lng_ref, lnb_ref, ws_ref, bst_ref, lbt_ref,
                hng_ref, wa_ref, wb_ref, wo_ref, o_ref,
                state_ref, qin_ref, kin_ref, qdec_ref, kdec_ref, v_ref, decay_ref, gs_ref,
                gu_ref, gv_ref, tga_ref, tgb_ref, a_ref, og_ref, mg_ref, *, first_in_step):
    T, D = x_ref.shape
    seg = win_ref.shape[1] // N_IN_SEGMENTS
    W = MIXER_COL_SLAB
    n_slabs = seg // W
    C = HGRN_CHUNK
    n_chunks = T // C

    if first_in_step:
        @pl.when(pl.program_id(1) == 0)
        def _():
            state_ref[...] = jnp.zeros_like(state_ref)

    h = _rms_norm(x_ref[...], nmg_ref[...]).astype(BF16)

    def proj(i, j):
        c0 = i * seg + j * W
        return _dot(h, _unpack_rows(win_ref[:, c0:c0 + W]))

    row = lax.broadcasted_iota(jnp.int32, (T, T), 0)
    col = lax.broadcasted_iota(jnp.int32, (T, T), 1)
    same_chunk = (row >> HGRN_CHUNK_SHIFT) == (col >> HGRN_CHUNK_SHIFT)
    causal = jnp.logical_and(same_chunk, col <= row)
    tri = jnp.where(causal, 1.0, 0.0).astype(BF16)

    lbt = lbt_ref[...]
    lbe = jnp.exp(lbt - jnp.max(lbt, axis=0, keepdims=True))
    lb = lbe[0:1, :] / jnp.sum(lbe, axis=0, keepdims=True)

    gv_sum = jnp.zeros((T, 1), F32)
    f_logits = [proj(3, j) for j in range(n_slabs)]
    gate_vals = []
    for j in range(n_slabs):
        lb_j = lb[:, j * W:(j + 1) * W]
        c1 = 0.5 * (1.0 - lb_j)
        f = (lb_j + c1) + c1 * jnp.tanh(0.5 * f_logits[j])
        log_f = jnp.log(f)
        lf_hi = log_f.astype(BF16)
        lf_lo = (log_f - lf_hi.astype(F32)).astype(BF16)
        gate_vals.append((1.0 - f, _dot(tri, lf_hi) + _dot(tri, lf_lo), proj(2, j) * HGRN_SCALE))
    for j in range(n_slabs):
        js = slice(j * W, (j + 1) * W)
        kk, A, q = gate_vals[j]
        a_mid_rows, e_mid_rows, e_lm_rows = [], [], []
        for c in range(n_chunks):
            a_mid = A[c * C + C // 2 - 1:c * C + C // 2, :]
            a_last = A[c * C + C - 1:c * C + C, :]
            a_mid_rows.append(jnp.broadcast_to(a_mid, (C, W)))
            e_mid_rows.append(jnp.broadcast_to(jnp.exp(a_mid), (C, W)))
            e_lm_rows.append(jnp.broadcast_to(jnp.exp(a_last - a_mid), (C, W)))
            decay_ref[c:c + 1, js] = jnp.exp(a_last)
        rel = A - jnp.concatenate(a_mid_rows, axis=0)
        q_in = q * jnp.exp(rel)
        k_in = kk * jnp.exp(-rel)
        qdec_ref[:, js] = (q_in * jnp.concatenate(e_mid_rows, axis=0)).astype(BF16)
        kdec_ref[:, js] = (k_in * jnp.concatenate(e_lm_rows, axis=0)).astype(BF16)
        qin_ref[:, js] = q_in.astype(BF16)
        kin_ref[:, js] = k_in.astype(BF16)
        v_ref[:, js] = proj(4, j).astype(BF16)
        gs_ref[:, js] = _silu(proj(5, j))
        gu_ref[:, js] = _gelu_tanh(proj(0, j))
        gv = _gelu_tanh(proj(1, j))
        gv_ref[:, js] = gv
        gv_sum = gv_sum + jnp.sum(gv, axis=-1, keepdims=True)

    gate_slabs = [(tga_ref, 6, j) for j in range(n_slabs)] + [(tgb_ref, 7, j) for j in range(n_slabs)]

    def next_gate_slab():
        ref, i, j = gate_slabs.pop(0)
        ref[:, j * W:(j + 1) * W] = jnp.tanh(0.5 * proj(i, j))

    next_gate_slab()
    mu = gv_sum * (1.0 / seg)
    sq_sum = jnp.zeros((T, 1), F32)
    for j in range(n_slabs):
        cen = gv_ref[:, j * W:(j + 1) * W] - mu
        sq_sum = sq_sum + jnp.sum(cen * cen, axis=-1, keepdims=True)
    rstd = lax.rsqrt(sq_sum * (1.0 / seg) + NORM_EPS)
    c_row = lax.broadcasted_iota(jnp.int32, (GMLP_CHUNK, GMLP_CHUNK), 0)
    c_col = lax.broadcasted_iota(jnp.int32, (GMLP_CHUNK, GMLP_CHUNK), 1)
    gd = seg // GMLP_GROUPS

    head_vals = {}

    def head_scores(hd):
        ks = slice(hd * HGRN_DK, (hd + 1) * HGRN_DK)
        vs = slice(hd * HGRN_DV, (hd + 1) * HGRN_DV)
        raw = _dot_nt(qin_ref[:, ks], kin_ref[:, ks])
        d_state = [_dot_tn(v_ref[c * C:(c + 1) * C, vs], kdec_ref[c * C:(c + 1) * C, ks])
                   for c in range(n_chunks)]
        head_vals[hd] = (raw, d_state)

    def head_outputs(hd):
        ks = slice(hd * HGRN_DK, (hd + 1) * HGRN_DK)
        vs = slice(hd * HGRN_DV, (hd + 1) * HGRN_DV)
        raw, d_state = head_vals[hd]
        scores = jnp.where(causal, raw, 0.0).astype(BF16)
        o_intra = _dot(scores, v_ref[:, vs])
        st = state_ref[hd]
        o_inter = []
        for c in range(n_chunks):
            o_inter.append(_dot_nt(qdec_ref[c * C:(c + 1) * C, ks], st.astype(BF16)))
            st = st * decay_ref[c:c + 1, ks] + d_state[c]
        state_ref[hd] = st
        head_vals[hd] = (o_intra, o_inter)

    def head_norm(hd):
        vs = slice(hd * HGRN_DV, (hd + 1) * HGRN_DV)
        o_intra, o_inter = head_vals.pop(hd)
        o_h = o_intra + jnp.concatenate(o_inter, axis=0)
        o_h = o_h * lax.rsqrt(jnp.mean(o_h * o_h, axis=-1, keepdims=True) + NORM_EPS)
        og_ref[:, vs] = (o_h * hng_ref[:, vs] * gs_ref[:, vs]).astype(BF16)

    def gmlp_group(g):
        cs = slice(g * gd, (g + 1) * gd)
        w_g = jnp.where(c_col <= c_row, ws_ref[g], 0.0).astype(BF16)
        bias = bst_ref[:, g:g + 1]
        lnv = ((gv_ref[:, cs] - mu) * rstd * lng_ref[:, cs] + lnb_ref[:, cs]).astype(BF16)
        for n in range(T // GMLP_CHUNK):
            rs = slice(n * GMLP_CHUNK, (n + 1) * GMLP_CHUNK)
            mixed = _dot(w_g, lnv[rs, :]) + bias
            a_ref[rs, cs] = (gu_ref[rs, cs] * mixed).astype(BF16)

    for it in range(HGRN_HEADS + 2):
        if it < HGRN_HEADS:
            head_scores(it)
        if 0 <= it - 1 < HGRN_HEADS:
            head_outputs(it - 1)
        if 0 <= it - 2 < HGRN_HEADS:
            head_norm(it - 2)
        if it < GMLP_GROUPS:
            gmlp_group(it)
        if it % 2 == 1 and gate_slabs:
            next_gate_slab()
    assert not gate_slabs and not head_vals

    for j in range(D // W):
        js = slice(j * W, (j + 1) * W)
        y_a = _dot(a_ref[...], _unpack_rows(wa_ref[:, js]))
        y_b = _dot(og_ref[...], _unpack_rows(wb_ref[:, js]))
        gated = (y_a + tga_ref[:, js] * y_a) + (y_b + tgb_ref[:, js] * y_b)
        mg_ref[:, js] = (0.5 * gated).astype(BF16)
    for j in range(D // W):
        js = slice(j * W, (j + 1) * W)
        o_ref[:, js] = x_ref[:, js] + _dot(mg_ref[...], _unpack_rows(wo_ref[:, js]))


def _ffn_kernel(x_ref, xn_ref, nfg_ref, wgu_ref, wd_ref, nog_ref, o_ref, h_ref, act_ref, carry_ref):
    T = FFN_TOKEN_TILE
    tiles = [x_ref.at[pl.ds(t * T, T)] for t in range(FFN_TILES_PER_STEP)] + [xn_ref]
    for t in range(FFN_TILES_PER_STEP):
        _ffn_tile(tiles[t], tiles[t + 1], nfg_ref, wgu_ref, wd_ref, nog_ref,
                  o_ref.at[pl.ds(t * T, T)], h_ref, act_ref, carry_ref,
                  read_slot=t % 2, first_in_step=(t == 0))


def _ffn_tile(x_ref, xn_ref, nfg_ref, wgu_ref, wd_ref, nog_ref, o_ref, h_ref, act_ref, carry_ref,
              *, read_slot, first_in_step):
    d_ff = wd_ref.shape[0]
    Wc = FFN_COL_CHUNK
    n_carry = FFN_CARRIED_CHUNKS

    write_slot = 1 - read_slot

    def act_chunk(j):
        gate = _dot(h_ref[...], wgu_ref[:, j * Wc:(j + 1) * Wc])
        up = _dot(h_ref[...], wgu_ref[:, d_ff + j * Wc:d_ff + (j + 1) * Wc])
        return (_silu(gate) * up).astype(BF16)

    if first_in_step:
        @pl.when(pl.program_id(0) == 0)
        def _():
            h_ref[...] = _rms_norm(x_ref[...], nfg_ref[...]).astype(BF16)
            for j in range(n_carry):
                carry_ref[read_slot, :, j * Wc:(j + 1) * Wc] = act_chunk(j)

    for j in range(n_carry, d_ff // Wc):
        act_ref[:, (j - n_carry) * Wc:(j - n_carry + 1) * Wc] = act_chunk(j)
    h_ref[...] = _rms_norm(xn_ref[...], nfg_ref[...]).astype(BF16)
    part = wd_ref.shape[1] // n_carry
    y_parts = []
    for j in range(n_carry):
        act = jnp.concatenate([carry_ref[read_slot], act_ref[...]], axis=1)
        y_parts.append(_dot(act, wd_ref[:, j * part:(j + 1) * part]))
        carry_ref[write_slot, :, j * Wc:(j + 1) * Wc] = act_chunk(j)
    x2 = x_ref[...] + jnp.concatenate(y_parts, axis=1)
    o_ref[...] = _rms_norm(x2, nog_ref[...])


def _pack_call(*weights):
    k = weights[0].shape[0]
    rows = PACK_ROW_BLOCK
    assert all(w.shape[0] == k for w in weights) and k % rows == 0
    return pl.pallas_call(
        _pack_kernel,
        grid=(k // rows,),
        in_specs=[pl.BlockSpec((rows, w.shape[1]), lambda i: (i, 0)) for w in weights],
        out_specs=[pl.BlockSpec((rows // 2, w.shape[1]), lambda i: (i, 0)) for w in weights],
        out_shape=[jax.ShapeDtypeStruct((k // 2, w.shape[1]), jnp.uint32) for w in weights],
        compiler_params=pltpu.CompilerParams(
            dimension_semantics=("arbitrary",),
            vmem_limit_bytes=V7X_VMEM_LIMIT_BYTES),
        name="pack_weights",
    )(*weights)


def _convert_call(packed, plain):
    steps = CONVERT_GRID_STEPS
    weights = list(packed) + list(plain)
    blocks = [w.shape[0] // steps for w in weights]
    assert all(w.shape[0] % steps == 0 and b % 16 == 0 for w, b in zip(weights, blocks))
    out_specs, out_shape = [], []
    for idx, (w, b) in enumerate(zip(weights, blocks)):
        if idx < len(packed):
            out_specs.append(pl.BlockSpec((b // 2, w.shape[1]), lambda i: (i, 0)))
            out_shape.append(jax.ShapeDtypeStruct((w.shape[0] // 2, w.shape[1]), jnp.uint32))
        else:
            out_specs.append(pl.BlockSpec((b, w.shape[1]), lambda i: (i, 0)))
            out_shape.append(jax.ShapeDtypeStruct(w.shape, BF16))
    return pl.pallas_call(
        _pack_kernel,
        grid=(steps,),
        in_specs=[pl.BlockSpec((b, w.shape[1]), lambda i: (i, 0)) for w, b in zip(weights, blocks)],
        out_specs=out_specs,
        out_shape=out_shape,
        compiler_params=pltpu.CompilerParams(
            dimension_semantics=("arbitrary",),
            vmem_limit_bytes=V7X_VMEM_LIMIT_BYTES),
        name="convert_weights",
    )(*weights)


def _resident(shape):
    n = len(shape)
    return pl.BlockSpec(shape, lambda *_: (0,) * n, pipeline_mode=pl.Buffered(1))


def _mixer_call(x, nmg, w_in, lng, lnb, w_s, bst, lbt, hng, w_a, w_b, w_o):
    B, S, D = x.shape
    T = MIXER_SEQ_TILE
    seg = w_in.shape[1] // N_IN_SEGMENTS
    rows = T * MIXER_TILES_PER_STEP
    x_spec = pl.BlockSpec((None, rows, D), lambda b, s: (b, s, 0))
    params = (nmg, w_in, lng, lnb, w_s, bst, lbt, hng, w_a, w_b, w_o)
    return pl.pallas_call(
        _mixer_kernel,
        grid=(B, S // rows),
        in_specs=[x_spec] + [_resident(p.shape) for p in params],
        out_specs=x_spec,
        out_shape=jax.ShapeDtypeStruct(x.shape, x.dtype),
        scratch_shapes=[
            pltpu.VMEM((HGRN_HEADS, HGRN_DV, HGRN_DK), F32),
            pltpu.VMEM((T, seg), BF16),
            pltpu.VMEM((T, seg), BF16),
            pltpu.VMEM((T, seg), BF16),
            pltpu.VMEM((T, seg), BF16),
            pltpu.VMEM((T, seg), BF16),
            pltpu.VMEM((T // HGRN_CHUNK, seg), F32),
            pltpu.VMEM((T, seg), F32),
            pltpu.VMEM((T, seg), F32),
            pltpu.VMEM((T, seg), F32),
            pltpu.VMEM((T, D), F32),
            pltpu.VMEM((T, D), F32),
            pltpu.VMEM((T, seg), BF16),
            pltpu.VMEM((T, seg), BF16),
            pltpu.VMEM((T, D), BF16),
        ],
        compiler_params=pltpu.CompilerParams(
            dimension_semantics=("arbitrary", "arbitrary"),
            vmem_limit_bytes=V7X_VMEM_LIMIT_BYTES),
        name="mixer",
    )(x, *params)


def _ffn_call(x, nfg, w_gu, w_d, nog):
    N, D = x.shape
    T = FFN_TOKEN_TILE
    n_tiles = N // T
    n_steps = n_tiles // FFN_TILES_PER_STEP
    n_carried = FFN_CARRIED_CHUNKS * FFN_COL_CHUNK
    x_spec = pl.BlockSpec((T * FFN_TILES_PER_STEP, D), lambda i: (i, 0))
    x_next_spec = pl.BlockSpec(
        (T, D), lambda i: (jnp.minimum((i + 1) * FFN_TILES_PER_STEP, n_tiles - 1), 0))
    params = (nfg, w_gu, w_d, nog)
    return pl.pallas_call(
        _ffn_kernel,
        grid=(n_steps,),
        in_specs=[x_spec, x_next_spec] + [_resident(p.shape) for p in params],
        out_specs=x_spec,
        out_shape=jax.ShapeDtypeStruct(x.shape, x.dtype),
        scratch_shapes=[
            pltpu.VMEM((T, D), BF16),
            pltpu.VMEM((T, w_d.shape[0] - n_carried), BF16),
            pltpu.VMEM((2, T, n_carried), BF16),
        ],
        compiler_params=pltpu.CompilerParams(
            dimension_semantics=("arbitrary",),
            vmem_limit_bytes=V7X_VMEM_LIMIT_BYTES),
        name="ffn",
    )(x, x, *params)


@jax.jit
def kernel(x, norm_mix_g, w_in, gmlp_ln_g, gmlp_ln_b, gmlp_w_s, gmlp_b_s, hgrn_lb_table,
           hgrn_norm_g, w_branch_a, w_branch_b, w_out, norm_ffn_g, w_gate_up, w_down,
           norm_final_g):
    B, S, D = x.shape
    depth = w_in.shape[0]
    assert depth == 1 and hgrn_lb_table.shape[0] == depth + 1
    assert S % (MIXER_SEQ_TILE * MIXER_TILES_PER_STEP) == 0 and (B * S) % (FFN_TOKEN_TILE * FFN_TILES_PER_STEP) == 0
    assert FFN_TILES_PER_STEP == 2
    assert w_down.shape[1] % FFN_COL_CHUNK == 0 and D % MIXER_COL_SLAB == 0
    assert D % FFN_CARRIED_CHUNKS == 0
    l = 0
    w_in_p, w_a_p, w_b_p, w_o_p, w_gu_b, w_d_b = _convert_call(
        (w_in[l], w_branch_a[l], w_branch_b[l], w_out[l]), (w_gate_up[l], w_down[l]))
    x1 = _mixer_call(
        x, norm_mix_g[l][None, :], w_in_p,
        gmlp_ln_g[l][None, :], gmlp_ln_b[l][None, :], gmlp_w_s[l], gmlp_b_s[l].T,
        hgrn_lb_table, hgrn_norm_g[l][None, :], w_a_p, w_b_p, w_o_p)
    out = _ffn_call(
        x1.reshape(B * S, D), norm_ffn_g[l][None, :], w_gu_b, w_d_b, norm_final_g[None, :])
    return out.reshape(B, S, D)
```

```python
import math

import jax
import jax.numpy as jnp
from jax import lax
from jax.experimental import pallas as pl
from jax.experimental.pallas import tpu as pltpu

F32 = jnp.float32
BF16 = jnp.bfloat16

NORM_EPS = 1e-6
GMLP_GROUPS = 8
GMLP_CHUNK = 128
HGRN_HEADS = 8
HGRN_DK = 128
HGRN_DV = 128
HGRN_CHUNK = 64
HGRN_CHUNK_SHIFT = HGRN_CHUNK.bit_length() - 1
assert 1 << HGRN_CHUNK_SHIFT == HGRN_CHUNK
GELU_CUBIC_COEFF = 0.044715
HGRN_SCALE = HGRN_DK ** -0.5
N_IN_SEGMENTS = 8

MIXER_SEQ_TILE = 256
MIXER_TILES_PER_STEP = 2
FFN_TOKEN_TILE = 512
FFN_TILES_PER_STEP = 2
FFN_COL_CHUNK = 256
FFN_CARRIED_CHUNKS = 2
CONVERT_GRID_STEPS = 8
MIXER_COL_SLAB = 512
V7X_VMEM_LIMIT_BYTES = 56 * 1024 * 1024


def _silu(x):
    h = 0.5 * x
    return h + h * jnp.tanh(h)


def _gelu_tanh(x):
    c = math.sqrt(2.0 / math.pi)
    h = 0.5 * x
    return h + h * jnp.tanh(x * (c + (c * GELU_CUBIC_COEFF) * (x * x)))


def _rms_norm(x, gain):
    return x * lax.rsqrt(jnp.mean(x * x, axis=-1, keepdims=True) + NORM_EPS) * gain


def _dot(a, b):
    return jnp.dot(a, b, preferred_element_type=F32)


def _dot_nt(a, b):
    return lax.dot_general(a, b, (((1,), (1,)), ((), ())), preferred_element_type=F32)


def _dot_tn(a, b):
    return lax.dot_general(a, b, (((0,), (0,)), ((), ())), preferred_element_type=F32)


def _unpack_rows(packed):
    return pltpu.bitcast(packed, BF16)


def _pack_kernel(*refs):
    n = len(refs) // 2
    for w_ref, o_ref in zip(refs[:n], refs[n:]):
        w = w_ref[...].astype(BF16)
        o_ref[...] = pltpu.bitcast(w, jnp.uint32) if o_ref.dtype == jnp.uint32 else w


def _mixer_kernel(x_ref, *refs):
    params, o_ref, scratch = refs[:11], refs[11], refs[12:]
    T = MIXER_SEQ_TILE
    for sub in range(x_ref.shape[0] // T):
        rows = pl.ds(sub * T, T)
        _mixer_tile(x_ref.at[rows], *params, o_ref.at[rows], *scratch, first_in_step=(sub == 0))


def _mixer_tile(x_ref, nmg_ref, win_ref, lng_ref, lnb_ref, ws_ref, bst_ref, lbt_ref,
                hng_ref, wa_ref, wb_ref, wo_ref, o_ref,
                state_ref, qin_ref, kin_ref, qdec_ref, kdec_ref, v_ref, decay_ref, gs_ref,
                gu_ref, gv_ref, tga_ref, tgb_ref, a_ref, og_ref, mg_ref, *, first_in_step):
    T, D = x_ref.shape
    seg = win_ref.shape[1] // N_IN_SEGMENTS
    W = MIXER_COL_SLAB
    n_slabs = seg // W
    C = HGRN_CHUNK
    n_chunks = T // C

    if first_in_step:
        @pl.when(pl.program_id(1) == 0)
        def _():
            state_ref[...] = jnp.zeros_like(state_ref)

    h = _rms_norm(x_ref[...], nmg_ref[...]).astype(BF16)

    def proj(i, j):
        c0 = i * seg + j * W
        return _dot(h, _unpack_rows(win_ref[:, c0:c0 + W]))

    row = lax.broadcasted_iota(jnp.int32, (T, T), 0)
    col = lax.broadcasted_iota(jnp.int32, (T, T), 1)
    same_chunk = (row >> HGRN_CHUNK_SHIFT) == (col >> HGRN_CHUNK_SHIFT)
    causal = jnp.logical_and(same_chunk, col <= row)
    tri = jnp.where(causal, 1.0, 0.0).astype(BF16)

    lbt = lbt_ref[...]
    lbe = jnp.exp(lbt - jnp.max(lbt, axis=0, keepdims=True))
    lb = lbe[0:1, :] / jnp.sum(lbe, axis=0, keepdims=True)

    gv_sum = jnp.zeros((T, 1), F32)
    f_logits = [proj(3, j) for j in range(n_slabs)]
    gate_vals = []
    for j in range(n_slabs):
        lb_j = lb[:, j * W:(j + 1) * W]
        c1 = 0.5 * (1.0 - lb_j)
        f = (lb_j + c1) + c1 * jnp.tanh(0.5 * f_logits[j])
        log_f = jnp.log(f)
        lf_hi = log_f.astype(BF16)
        lf_lo = (log_f - lf_hi.astype(F32)).astype(BF16)
        gate_vals.append((1.0 - f, _dot(tri, lf_hi) + _dot(tri, lf_lo), proj(2, j) * HGRN_SCALE))
    for j in range(n_slabs):
        js = slice(j * W, (j + 1) * W)
        kk, A, q = gate_vals[j]
        a_mid_rows, e_mid_rows, e_lm_rows = [], [], []
        for c in range(n_chunks):
            a_mid = A[c * C + C // 2 - 1:c * C + C // 2, :]
            a_last = A[c * C + C - 1:c * C + C, :]
            a_mid_rows.append(jnp.broadcast_to(a_mid, (C, W)))
            e_mid_rows.append(jnp.broadcast_to(jnp.exp(a_mid), (C, W)))
            e_lm_rows.append(jnp.broadcast_to(jnp.exp(a_last - a_mid), (C, W)))
            decay_ref[c:c + 1, js] = jnp.exp(a_last)
        rel = A - jnp.concatenate(a_mid_rows, axis=0)
        q_in = q * jnp.exp(rel)
        k_in = kk * jnp.exp(-rel)
        qdec_ref[:, js] = (q_in * jnp.concatenate(e_mid_rows, axis=0)).astype(BF16)
        kdec_ref[:, js] = (k_in * jnp.concatenate(e_lm_rows, axis=0)).astype(BF16)
        qin_ref[:, js] = q_in.astype(BF16)
        kin_ref[:, js] = k_in.astype(BF16)
        v_ref[:, js] = proj(4, j).astype(BF16)
        gs_ref[:, js] = _silu(proj(5, j))
        gu_ref[:, js] = _gelu_tanh(proj(0, j))
        gv = _gelu_tanh(proj(1, j))
        gv_ref[:, js] = gv
        gv_sum = gv_sum + jnp.sum(gv, axis=-1, keepdims=True)

    gate_slabs = [(tga_ref, 6, j) for j in range(n_slabs)] + [(tgb_ref, 7, j) for j in range(n_slabs)]

    def next_gate_slab():
        ref, i, j = gate_slabs.pop(0)
        ref[:, j * W:(j + 1) * W] = jnp.tanh(0.5 * proj(i, j))

    next_gate_slab()
    mu = gv_sum * (1.0 / seg)
    sq_sum = jnp.zeros((T, 1), F32)
    for j in range(n_slabs):
        cen = gv_ref[:, j * W:(j + 1) * W] - mu
        sq_sum = sq_sum + jnp.sum(cen * cen, axis=-1, keepdims=True)
    rstd = lax.rsqrt(sq_sum * (1.0 / seg) + NORM_EPS)
    c_row = lax.broadcasted_iota(jnp.int32, (GMLP_CHUNK, GMLP_CHUNK), 0)
    c_col = lax.broadcasted_iota(jnp.int32, (GMLP_CHUNK, GMLP_CHUNK), 1)
    gd = seg // GMLP_GROUPS

    head_vals = {}

    def head_scores(hd):
        ks = slice(hd * HGRN_DK, (hd + 1) * HGRN_DK)
        vs = slice(hd * HGRN_DV, (hd + 1) * HGRN_DV)
        raw = _dot_nt(qin_ref[:, ks], kin_ref[:, ks])
        d_state = [_dot_tn(v_ref[c * C:(c + 1) * C, vs], kdec_ref[c * C:(c + 1) * C, ks])
                   for c in range(n_chunks)]
        head_vals[hd] = (raw, d_state)

    def head_outputs(hd):
        ks = slice(hd * HGRN_DK, (hd + 1) * HGRN_DK)
        vs = slice(hd * HGRN_DV, (hd + 1) * HGRN_DV)
        raw, d_state = head_vals[hd]
        scores = jnp.where(causal, raw, 0.0).astype(BF16)
        o_intra = _dot(scores, v_ref[:, vs])
        st = state_ref[hd]
        o_inter = []
        for c in range(n_chunks):
            o_inter.append(_dot_nt(qdec_ref[c * C:(c + 1) * C, ks], st.astype(BF16)))
            st = st * decay_ref[c:c + 1, ks] + d_state[c]
        state_ref[hd] = st
        head_vals[hd] = (o_intra, o_inter)

    def head_norm(hd):
        vs = slice(hd * HGRN_DV, (hd + 1) * HGRN_DV)
        o_intra, o_inter = head_vals.pop(hd)
        o_h = o_intra + jnp.concatenate(o_inter, axis=0)
        o_h = o_h * lax.rsqrt(jnp.mean(o_h * o_h, axis=-1, keepdims=True) + NORM_EPS)
        og_ref[:, vs] = (o_h * hng_ref[:, vs] * gs_ref[:, vs]).astype(BF16)

    def gmlp_group(g):
        cs = slice(g * gd, (g + 1) * gd)
        w_g = jnp.where(c_col <= c_row, ws_ref[g], 0.0).astype(BF16)
        bias = bst_ref[:, g:g + 1]
        lnv = ((gv_ref[:, cs] - mu) * rstd * lng_ref[:, cs] + lnb_ref[:, cs]).astype(BF16)
        for n in range(T // GMLP_CHUNK):
            rs = slice(n * GMLP_CHUNK, (n + 1) * GMLP_CHUNK)
            mixed = _dot(w_g, lnv[rs, :]) + bias
            a_ref[rs, cs] = (gu_ref[rs, cs] * mixed).astype(BF16)

    for it in range(HGRN_HEADS + 2):
        if it < HGRN_HEADS:
            head_scores(it)
        if 0 <= it - 1 < HGRN_HEADS:
            head_outputs(it - 1)
        if 0 <= it - 2 < HGRN_HEADS:
            head_norm(it - 2)
        if it < GMLP_GROUPS:
            gmlp_group(it)
        if it % 2 == 1 and gate_slabs:
            next_gate_slab()
    assert not gate_slabs and not head_vals

    for j in range(D // W):
        js = slice(j * W, (j + 1) * W)
        y_a = _dot(a_ref[...], _unpack_rows(wa_ref[:, js]))
        y_b = _dot(og_ref[...], _unpack_rows(wb_ref[:, js]))
        gated = (y_a + tga_ref[:, js] * y_a) + (y_b + tgb_ref[:, js] * y_b)
        mg_ref[:, js] = (0.5 * gated).astype(BF16)
    for j in range(D // W):
        js = slice(j * W, (j + 1) * W)
        o_ref[:, js] = x_ref[:, js] + _dot(mg_ref[...], _unpack_rows(wo_ref[:, js]))


def _ffn_kernel(x_ref, xn_ref, nfg_ref, wgu_ref, wd_ref, nog_ref, o_ref, h_ref, act_ref, carry_ref):
    T = FFN_TOKEN_TILE
    tiles = [x_ref.at[pl.ds(t * T, T)] for t in range(FFN_TILES_PER_STEP)] + [xn_ref]
    for t in range(FFN_TILES_PER_STEP):
        _ffn_tile(tiles[t], tiles[t + 1], nfg_ref, wgu_ref, wd_ref, nog_ref,
                  o_ref.at[pl.ds(t * T, T)], h_ref, act_ref, carry_ref,
                  read_slot=t % 2, first_in_step=(t == 0))


def _ffn_tile(x_ref, xn_ref, nfg_ref, wgu_ref, wd_ref, nog_ref, o_ref, h_ref, act_ref, carry_ref,
              *, read_slot, first_in_step):
    d_ff = wd_ref.shape[0]
    Wc = FFN_COL_CHUNK
    n_carry = FFN_CARRIED_CHUNKS

    write_slot = 1 - read_slot

    def act_chunk(j):
        gate = _dot(h_ref[...], wgu_ref[:, j * Wc:(j + 1) * Wc])
        up = _dot(h_ref[...], wgu_ref[:, d_ff + j * Wc:d_ff + (j + 1) * Wc])
        return (_silu(gate) * up).astype(BF16)

    if first_in_step:
        @pl.when(pl.program_id(0) == 0)
        def _():
            h_ref[...] = _rms_norm(x_ref[...], nfg_ref[...]).astype(BF16)
            for j in range(n_carry):
                carry_ref[read_slot, :, j * Wc:(j + 1) * Wc] = act_chunk(j)

    for j in range(n_carry, d_ff // Wc):
        act_ref[:, (j - n_carry) * Wc:(j - n_carry + 1) * Wc] = act_chunk(j)
    h_ref[...] = _rms_norm(xn_ref[...], nfg_ref[...]).astype(BF16)
    part = wd_ref.shape[1] // n_carry
    y_parts = []
    for j in range(n_carry):
        act = jnp.concatenate([carry_ref[read_slot], act_ref[...]], axis=1)
        y_parts.append(_dot(act, wd_ref[:, j * part:(j + 1) * part]))
        carry_ref[write_slot, :, j * Wc:(j + 1) * Wc] = act_chunk(j)
    x2 = x_ref[...] + jnp.concatenate(y_parts, axis=1)
    o_ref[...] = _rms_norm(x2, nog_ref[...])


def _convert_call(packed, plain):
    steps = CONVERT_GRID_STEPS
    weights = list(packed) + list(plain)
    blocks = [w.shape[0] // steps for w in weights]
    assert all(w.shape[0] % steps == 0 and b % 16 == 0 for w, b in zip(weights, blocks))
    out_specs, out_shape = [], []
    for idx, (w, b) in enumerate(zip(weights, blocks)):
        if idx < len(packed):
            out_specs.append(pl.BlockSpec((b // 2, w.shape[1]), lambda i: (i, 0)))
            out_shape.append(jax.ShapeDtypeStruct((w.shape[0] // 2, w.shape[1]), jnp.uint32))
        else:
            out_specs.append(pl.BlockSpec((b, w.shape[1]), lambda i: (i, 0)))
            out_shape.append(jax.ShapeDtypeStruct(w.shape, BF16))
    return pl.pallas_call(
        _pack_kernel,
        grid=(steps,),
        in_specs=[pl.BlockSpec((b, w.shape[1]), lambda i: (i, 0)) for w, b in zip(weights, blocks)],
        out_specs=out_specs,
        out_shape=out_shape,
        compiler_params=pltpu.CompilerParams(
            dimension_semantics=("arbitrary",),
            vmem_limit_bytes=V7X_VMEM_LIMIT_BYTES),
        name="convert_weights",
    )(*weights)


def _resident(shape):
    n = len(shape)
    return pl.BlockSpec(shape, lambda *_: (0,) * n, pipeline_mode=pl.Buffered(1))


def _mixer_call(x, nmg, w_in, lng, lnb, w_s, bst, lbt, hng, w_a, w_b, w_o):
    B, S, D = x.shape
    T = MIXER_SEQ_TILE
    seg = w_in.shape[1] // N_IN_SEGMENTS
    rows = T * MIXER_TILES_PER_STEP
    x_spec = pl.BlockSpec((None, rows, D), lambda b, s: (b, s, 0))
    params = (nmg, w_in, lng, lnb, w_s, bst, lbt, hng, w_a, w_b, w_o)
    return pl.pallas_call(
        _mixer_kernel,
        grid=(B, S // rows),
        in_specs=[x_spec] + [_resident(p.shape) for p in params],
        out_specs=x_spec,
        out_shape=jax.ShapeDtypeStruct(x.shape, x.dtype),
        scratch_shapes=[
            pltpu.VMEM((HGRN_HEADS, HGRN_DV, HGRN_DK), F32),
            pltpu.VMEM((T, seg), BF16),
            pltpu.VMEM((T, seg), BF16),
            pltpu.VMEM((T, seg), BF16),
            pltpu.VMEM((T, seg), BF16),
            pltpu.VMEM((T, seg), BF16),
            pltpu.VMEM((T // HGRN_CHUNK, seg), F32),
            pltpu.VMEM((T, seg), F32),
            pltpu.VMEM((T, seg), F32),
            pltpu.VMEM((T, seg), F32),
            pltpu.VMEM((T, D), F32),
            pltpu.VMEM((T, D), F32),
            pltpu.VMEM((T, seg), BF16),
            pltpu.VMEM((T, seg), BF16),
            pltpu.VMEM((T, D), BF16),
        ],
        compiler_params=pltpu.CompilerParams(
            dimension_semantics=("arbitrary", "arbitrary"),
            vmem_limit_bytes=V7X_VMEM_LIMIT_BYTES),
        name="mixer",
    )(x, *params)


def _ffn_call(x, nfg, w_gu, w_d, nog):
    N, D = x.shape
    T = FFN_TOKEN_TILE
    n_tiles = N // T
    n_steps = n_tiles // FFN_TILES_PER_STEP
    n_carried = FFN_CARRIED_CHUNKS * FFN_COL_CHUNK
    x_spec = pl.BlockSpec((T * FFN_TILES_PER_STEP, D), lambda i: (i, 0))
    x_next_spec = pl.BlockSpec(
        (T, D), lambda i: (jnp.minimum((i + 1) * FFN_TILES_PER_STEP, n_tiles - 1), 0))
    params = (nfg, w_gu, w_d, nog)
    return pl.pallas_call(
        _ffn_kernel,
        grid=(n_steps,),
        in_specs=[x_spec, x_next_spec] + [_resident(p.shape) for p in params],
        out_specs=x_spec,
        out_shape=jax.ShapeDtypeStruct(x.shape, x.dtype),
        scratch_shapes=[
            pltpu.VMEM((T, D), BF16),
            pltpu.VMEM((T, w_d.shape[0] - n_carried), BF16),
            pltpu.VMEM((2, T, n_carried), BF16),
        ],
        compiler_params=pltpu.CompilerParams(
            dimension_semantics=("arbitrary",),
            vmem_limit_bytes=V7X_VMEM_LIMIT_BYTES),
        name="ffn",
    )(x, x, *params)


@jax.jit
def kernel(x, norm_mix_g, w_in, gmlp_ln_g, gmlp_ln_b, gmlp_w_s, gmlp_b_s, hgrn_lb_table,
           hgrn_norm_g, w_branch_a, w_branch_b, w_out, norm_ffn_g, w_gate_up, w_down,
           norm_final_g):
    B, S, D = x.shape
    depth = w_in.shape[0]
    assert depth == 1 and hgrn_lb_table.shape[0] == depth + 1
    assert S % (MIXER_SEQ_TILE * MIXER_TILES_PER_STEP) == 0 and (B * S) % (FFN_TOKEN_TILE * FFN_TILES_PER_STEP) == 0
    assert FFN_TILES_PER_STEP == 2
    assert w_down.shape[1] % FFN_COL_CHUNK == 0 and D % MIXER_COL_SLAB == 0
    assert D % FFN_CARRIED_CHUNKS == 0
    l = 0
    w_in_p, w_a_p, w_b_p, w_o_p, w_gu_b, w_d_b = _convert_call(
        (w_in[l], w_branch_a[l], w_branch_b[l], w_out[l]), (w_gate_up[l], w_down[l]))
    x1 = _mixer_call(
        x, norm_mix_g[l][None, :], w_in_p,
        gmlp_ln_g[l][None, :], gmlp_ln_b[l][None, :], gmlp_w_s[l], gmlp_b_s[l].T,
        hgrn_lb_table, hgrn_norm_g[l][None, :], w_a_p, w_b_p, w_o_p)
    out = _ffn_call(
        x1.reshape(B * S, D), norm_ffn_g[l][None, :], w_gu_b, w_d_b, norm_final_g[None, :])
    return out.reshape(B, S, D)
```
